```python
import math
import jax, jax.numpy as jnp
from jax import lax
import numpy as np

D_MODEL = 1024
BATCH = 8
SEQ = 2048
DEPTH = 4

D_MIX = D_MODEL
N_DIRS = 2
DN_HEADS = 4
DN_DK = 128
DN_DV = 128
DN_KEY = DN_HEADS * DN_DK
DN_VAL = DN_HEADS * DN_DV
DN_CONV = 5
DN_CHUNK = 64
HY_WIDTH = D_MIX - DN_VAL
HY_ORDER = 2
HY_SHORT = 3
HY_EMB = 33
HY_FILTER_HIDDEN = 64
HY_DIRS = 2
HY_FAST_DECAY_PCT = 0.3
HY_SLOW_DECAY_PCT = 1.5
HY_DECAY_TARGET = 1e-2
D_FF = 2816
RMS_EPS = 1e-6

_O_Q = DN_KEY
_O_K = 2 * DN_KEY
_O_V = _O_K + DN_VAL
_O_Z = _O_V + DN_VAL
_O_B = _O_Z + N_DIRS * DN_HEADS
_O_A = _O_B + N_DIRS * DN_HEADS
D_IN = _O_A + 3 * HY_WIDTH
IN_SPLITS = (_O_Q, _O_K, _O_V, _O_Z, _O_B, _O_A)

kernel_name = "hybrid_deltanet_hyena_macaron_encoder"


def _rmsnorm(x, w):
    xf = x.astype(jnp.float32)
    y = xf * lax.rsqrt(jnp.mean(xf * xf, axis=-1, keepdims=True) + RMS_EPS)
    return (y * w.astype(jnp.float32)).astype(x.dtype)


def _l2norm(t):
    return t * lax.rsqrt(jnp.sum(t * t, axis=-1, keepdims=True) + 1e-6)


def _swiglu(h, w_gate, w_up, w_down):
    return (jax.nn.silu(h @ w_gate) * (h @ w_up)) @ w_down


def _dwconv(x, w):
    k = w.shape[0]
    return lax.conv_general_dilated(
        x, w[:, None, :].astype(x.dtype), window_strides=(1,),
        padding=[(k // 2, k // 2)], dimension_numbers=("NWC", "WIO", "NWC"),
        feature_group_count=x.shape[-1])


def _gated_delta_rule(q, k, v, beta, g):
    b, h, l, dk = q.shape
    dv = v.shape[-1]
    c = DN_CHUNK
    n = l // c
    q = q.reshape(b, h, n, c, dk)
    k = k.reshape(b, h, n, c, dk)
    v = v.reshape(b, h, n, c, dv)
    beta = beta.reshape(b, h, n, c)
    g = jnp.cumsum(g.reshape(b, h, n, c), axis=-1)
    incl = jnp.tril(jnp.ones((c, c), bool))
    strict = jnp.tril(jnp.ones((c, c), bool), -1)
    diff = g[..., :, None] - g[..., None, :]
    decay = jnp.where(incl, jnp.exp(jnp.where(incl, diff, 0.0)), 0.0)
    kb = k * beta[..., None]
    a_mat = jnp.where(strict, jnp.einsum("bhnid,bhnjd->bhnij", kb, k) * decay, 0.0)
    a_mat = a_mat + jnp.eye(c, dtype=q.dtype)
    rhs = jnp.concatenate([v * beta[..., None], kb * jnp.exp(g)[..., None]], axis=-1)
    sol = lax.linalg.triangular_solve(a_mat, rhs, left_side=True, lower=True,
                                      unit_diagonal=True)
    u, w = sol[..., :dv], sol[..., dv:]
    attn = jnp.einsum("bhnid,bhnjd->bhnij", q, k) * decay
    g_last = g[..., -1]
    q_dec = q * jnp.exp(g)[..., None]
    k_dec = k * jnp.exp(g_last[..., None] - g)[..., None]

    def step(state, inp):
        q_i, k_i, u_i, w_i, attn_i, gl_i = inp
        v_new = u_i - jnp.einsum("bhcd,bhde->bhce", w_i, state)
        o_i = (jnp.einsum("bhcd,bhde->bhce", q_i, state)
               + jnp.einsum("bhcs,bhse->bhce", attn_i, v_new))
        state = (state * jnp.exp(gl_i)[..., None, None]
                 + jnp.einsum("bhcd,bhce->bhde", k_i, v_new))
        return state, o_i

    xs = tuple(jnp.moveaxis(t, 2, 0) for t in (q_dec, k_dec, u, w, attn, g_last))
    state0 = jnp.zeros((b, h, dk, dv), q.dtype)
    _, o = lax.scan(step, state0, xs)
    return jnp.moveaxis(o, 0, 2).reshape(b, h, l, dv)


def _deltanet_group(q, k, v, z, beta_logit, alpha, a_log, dt_bias, conv_w, norm_w):
    f32 = jnp.float32
    b, l, _ = q.shape
    qkv = jax.nn.silu(_dwconv(jnp.concatenate([q, k, v], axis=-1), conv_w))
    q, k, v = jnp.split(qkv, [DN_KEY, 2 * DN_KEY], axis=-1)

    def to_heads(t, d):
        return t.reshape(b, l, DN_HEADS, d).transpose(0, 2, 1, 3).astype(f32)

    q = _l2norm(to_heads(q, DN_DK)) * (DN_DK ** -0.5)
    k = _l2norm(to_heads(k, DN_DK))
    v = to_heads(v, DN_DV)
    beta = jax.nn.sigmoid(beta_logit.astype(f32)).reshape(b, l, N_DIRS, DN_HEADS).transpose(2, 0, 3, 1)
    a_in = alpha.astype(f32).reshape(b, l, N_DIRS, DN_HEADS).transpose(2, 0, 3, 1)
    g = -jnp.exp(a_log.astype(f32))[:, None, :, None] * jax.nn.softplus(
        a_in + dt_bias.astype(f32)[:, None, :, None])
    o_fwd = _gated_delta_rule(q, k, v, beta[0], g[0])
    flip = lambda t: jnp.flip(t, axis=2)
    o_bwd = flip(_gated_delta_rule(flip(q), flip(k), flip(v), flip(beta[1]), flip(g[1])))
    o = (o_fwd + o_bwd).transpose(0, 2, 1, 3)
    o = _rmsnorm(o, norm_w) * jax.nn.silu(z.reshape(b, l, DN_HEADS, DN_DV).astype(f32))
    return o.reshape(b, l, DN_VAL).astype(z.dtype)


def _hyena_filter_spectrum(l, w1, b1, w2, b2, w3, b3, freq, wout):
    f32 = jnp.float32
    t = jnp.linspace(0.0, 1.0, l, dtype=f32)[:, None]
    bands = (HY_EMB - 1) // 2
    ang = ((2.0 * math.pi / l) * jnp.arange(l, dtype=f32)[:, None]
           * jnp.linspace(1e-4, bands - 1, bands, dtype=f32)[None, :])
    feats = jnp.concatenate([t, jnp.cos(ang), -jnp.sin(ang)], axis=-1)
    fr = freq.astype(f32)
    hdn = jnp.sin(fr * (feats @ w1.astype(f32) + b1.astype(f32)))
    hdn = jnp.sin(fr * (hdn @ w2.astype(f32) + b2.astype(f32)))
    hdn = jnp.sin(fr * (hdn @ w3.astype(f32) + b3.astype(f32)))
    filt = (hdn @ wout.astype(f32)).reshape(l, HY_DIRS, HY_ORDER, HY_WIDTH)
    max_decay = math.log(HY_DECAY_TARGET) / HY_FAST_DECAY_PCT
    min_decay = math.log(HY_DECAY_TARGET) / HY_SLOW_DECAY_PCT
    deltas = jnp.abs(jnp.linspace(min_decay, max_decay, HY_WIDTH, dtype=f32))
    filt = filt * jnp.exp(-t * deltas)[:, None, None, :]
    fwd, bwd = filt[:, 0], filt[:, 1]
    two_sided = jnp.concatenate(
        [fwd, jnp.zeros((1, HY_ORDER, HY_WIDTH), f32), jnp.flip(bwd[: l - 1], axis=0)], axis=0)
    return jnp.fft.rfft(two_sided, axis=0)


def _fft_conv(z, kspec):
    l = z.shape[1]
    zs = jnp.fft.rfft(z, n=2 * l, axis=1)
    return jnp.fft.irfft(zs * kspec[None], n=2 * l, axis=1)[:, :l]


def _hyena_group(u, conv_w, conv_b, w1, b1, w2, b2, w3, b3, freq, wout, skip, norm_w):
    f32 = jnp.float32
    l = u.shape[1]
    u = _dwconv(u, conv_w) + conv_b
    v, x1, x2 = [t.astype(f32) for t in jnp.split(u, 3, axis=-1)]
    kspec = _hyena_filter_spectrum(l, w1, b1, w2, b2, w3, b3, freq, wout)
    skip = skip.astype(f32)
    zz = v
    for o, gate in enumerate((x1, x2)):
        zz = gate * (_fft_conv(zz, kspec[:, o]) + skip[o] * zz)
    return _rmsnorm(zz, norm_w).astype(u.dtype)


def setup_inputs(seed: int = 0) -> dict:
    key = jax.random.key(seed)
    ks = iter(jax.random.split(key, 40))
    f32 = jnp.float32

    def nrm(shape, scale):
        return scale * jax.random.normal(next(ks), shape, f32)

    def gain(shape):
        return 1.0 + 0.02 * jax.random.normal(next(ks), shape, f32)

    x = jax.random.normal(next(ks), (BATCH, SEQ, D_MODEL), f32)
    a_log = jnp.log(jax.random.uniform(next(ks), (DEPTH, N_DIRS, DN_HEADS), f32, 1.0, 16.0))
    dt = jnp.exp(jax.random.uniform(next(ks), (DEPTH, N_DIRS, DN_HEADS), f32,
                                    math.log(1e-3), math.log(1e-1)))
    dt_bias = dt + jnp.log(-jnp.expm1(-dt))
    hf = HY_FILTER_HIDDEN
    return {
        "x": x,
        "ffn1_norm": gain((DEPTH, D_MODEL)),
        "ffn1_w_gate": nrm((DEPTH, D_MODEL, D_FF), D_MODEL ** -0.5),
        "ffn1_w_up": nrm((DEPTH, D_MODEL, D_FF), D_MODEL ** -0.5),
        "ffn1_w_down": nrm((DEPTH, D_FF, D_MODEL), D_FF ** -0.5),
        "mix_norm": gain((DEPTH, D_MODEL)),
        "w_in": nrm((DEPTH, D_MODEL, D_IN), D_MODEL ** -0.5),
        "dn_conv": nrm((DEPTH, DN_CONV, 2 * DN_KEY + DN_VAL), DN_CONV ** -0.5),
        "dn_a_log": a_log,
        "dn_dt_bias": dt_bias,
        "dn_norm": gain((DEPTH, DN_DV)),
        "hy_conv": nrm((DEPTH, HY_SHORT, 3 * HY_WIDTH), HY_SHORT ** -0.5),
        "hy_conv_bias": nrm((DEPTH, 3 * HY_WIDTH), 0.02),
        "hy_f_w1": nrm((DEPTH, HY_EMB, hf), HY_EMB ** -0.5),
        "hy_f_b1": nrm((DEPTH, hf), 0.1),
        "hy_f_w2": nrm((DEPTH, hf, hf), hf ** -0.5),
        "hy_f_b2": nrm((DEPTH, hf), 0.1),
        "hy_f_w3": nrm((DEPTH, hf, hf), hf ** -0.5),
        "hy_f_b3": nrm((DEPTH, hf), 0.1),
        "hy_f_freq": gain((DEPTH, hf)),
        "hy_f_wout": nrm((DEPTH, hf, HY_DIRS * HY_ORDER * HY_WIDTH), 0.05 * hf ** -0.5),
        "hy_skip": nrm((DEPTH, HY_ORDER, HY_WIDTH), 1.0),
        "hy_norm": gain((DEPTH, HY_WIDTH)),
        "w_out": nrm((DEPTH, D_MIX, D_MODEL), D_MIX ** -0.5),
        "ffn2_norm": gain((DEPTH, D_MODEL)),
        "ffn2_w_gate": nrm((DEPTH, D_MODEL, D_FF), D_MODEL ** -0.5),
        "ffn2_w_up": nrm((DEPTH, D_MODEL, D_FF), D_MODEL ** -0.5),
        "ffn2_w_down": nrm((DEPTH, D_FF, D_MODEL), D_FF ** -0.5),
        "final_norm": gain((D_MODEL,)),
    }


def reference(x, ffn1_norm, ffn1_w_gate, ffn1_w_up, ffn1_w_down, mix_norm, w_in,
              dn_conv, dn_a_log, dn_dt_bias, dn_norm, hy_conv, hy_conv_bias,
              hy_f_w1, hy_f_b1, hy_f_w2, hy_f_b2, hy_f_w3, hy_f_b3, hy_f_freq,
              hy_f_wout, hy_skip, hy_norm, w_out, ffn2_norm, ffn2_w_gate,
              ffn2_w_up, ffn2_w_down, final_norm):
    for i in range(DEPTH):
        x = x + 0.5 * _swiglu(_rmsnorm(x, ffn1_norm[i]), ffn1_w_gate[i], ffn1_w_up[i], ffn1_w_down[i])
        proj = _rmsnorm(x, mix_norm[i]) @ w_in[i]
        q, k, v, z, beta_logit, alpha, hy_in = jnp.split(proj, IN_SPLITS, axis=-1)
        o_dn = _deltanet_group(q, k, v, z, beta_logit, alpha, dn_a_log[i], dn_dt_bias[i],
                               dn_conv[i], dn_norm[i])
        o_hy = _hyena_group(hy_in, hy_conv[i], hy_conv_bias[i], hy_f_w1[i], hy_f_b1[i],
                            hy_f_w2[i], hy_f_b2[i], hy_f_w3[i], hy_f_b3[i], hy_f_freq[i],
                            hy_f_wout[i], hy_skip[i], hy_norm[i])
        x = x + jnp.concatenate([o_dn, o_hy], axis=-1) @ w_out[i]
        x = x + 0.5 * _swiglu(_rmsnorm(x, ffn2_norm[i]), ffn2_w_gate[i], ffn2_w_up[i], ffn2_w_down[i])
    return _rmsnorm(x, final_norm)
```

```python
import functools
import math

import jax
import jax.numpy as jnp
from jax import lax
from jax.experimental import pallas as pl
from jax.experimental.pallas import tpu as pltpu

F32 = jnp.float32
BF16 = jnp.bfloat16
HIGHEST = lax.Precision.HIGHEST

D_MODEL = 1024
DEPTH = 4
N_DIRS = 2
DN_HEADS = 4
DN_DK = 128
DN_DV = 128
DN_KEY = DN_HEADS * DN_DK
DN_VAL = DN_HEADS * DN_DV
DN_CONV = 5
HY_WIDTH = D_MODEL - DN_VAL
HY_ORDER = 2
HY_SHORT = 3
HY_EMB = 33
HY_HID = 64
HY_DIRS = 2
HY_FAST_DECAY_PCT = 0.3
HY_SLOW_DECAY_PCT = 1.5
HY_DECAY_TARGET = 1e-2
D_FF = 2816
RMS_EPS = 1e-6

LANES = 128
SUBLANES = 8
MXU_DIM = 256
VMEM_LIMIT_BYTES = 58 * 1024 * 1024

N_MAIN = 2 * DN_KEY + 2 * DN_VAL + 3 * HY_WIDTH
COL_TILE = 512
N_COL_TILES = N_MAIN // COL_TILE
HY_COL0 = (2 * DN_KEY + 2 * DN_VAL) // COL_TILE
FF_TILE = MXU_DIM
N_FF_TILES = D_FF // FF_TILE
DELTA_CHUNK = MXU_DIM
HY_CT = MXU_DIM
HY_ROWS = 256
PAD_HID = LANES


def _params(n_grid):
    return pltpu.CompilerParams(
        dimension_semantics=("arbitrary",) * n_grid,
        vmem_limit_bytes=VMEM_LIMIT_BYTES)


def _resident(block, imap):
    return pl.BlockSpec(block, imap, pipeline_mode=pl.Buffered(1))


def _dot(a, b):
    return jnp.dot(a, b, preferred_element_type=F32)


def _silu(x):
    return x * jax.nn.sigmoid(x)


def _ffn_body(layer_ref, x_ref, nw_ref, wg_ref, wu_ref, wd_ref, o_ref, h_ref, a_ref):
    del layer_ref
    x = x_ref[...]
    ms = jnp.mean(x * x, axis=-1, keepdims=True)
    h_ref[...] = (x * lax.rsqrt(ms + RMS_EPS) * nw_ref[...]).astype(BF16)
    for f in range(N_FF_TILES):
        cols = slice(f * FF_TILE, (f + 1) * FF_TILE)
        g = _dot(h_ref[...], wg_ref[:, cols])
        u = _dot(h_ref[...], wu_ref[:, cols])
        a_ref[:, cols] = (_silu(g) * u).astype(BF16)
    o_ref[...] = x_ref[...] + 0.5 * _dot(a_ref[...], wd_ref[...])


def _ffn(layer, x2d, norm_w, wg, wu, wd, *, tm):
    m = x2d.shape[0]
    grid_spec = pltpu.PrefetchScalarGridSpec(
        num_scalar_prefetch=1,
        grid=(m // tm,),
        in_specs=[
            pl.BlockSpec((tm, D_MODEL), lambda i, l: (i, 0)),
            pl.BlockSpec((None, 1, D_MODEL), lambda i, l: (l[0], 0, 0)),
            _resident((None, D_MODEL, D_FF), lambda i, l: (l[0], 0, 0)),
            _resident((None, D_MODEL, D_FF), lambda i, l: (l[0], 0, 0)),
            _resident((None, D_FF, D_MODEL), lambda i, l: (l[0], 0, 0)),
        ],
        out_specs=pl.BlockSpec((tm, D_MODEL), lambda i, l: (i, 0)),
        scratch_shapes=[pltpu.VMEM((tm, D_MODEL), BF16), pltpu.VMEM((tm, D_FF), BF16)],
    )
    return pl.pallas_call(
        _ffn_body, grid_spec=grid_spec,
        out_shape=jax.ShapeDtypeStruct((m, D_MODEL), F32),
        compiler_params=_params(1), name="ffn",
    )(layer, x2d, norm_w, wg, wu, wd)


def _softplus(x):
    return jnp.maximum(x, 0.0) + jnp.log1p(jnp.exp(-jnp.abs(x)))


def _mixin_body(layer_ref, x_ref, nw_ref, wm_ref, wba_ref, dnc_ref, hyc_ref, hyb_ref,
                alog_ref, dtb_ref, main_ref, st_ref, gct_ref, h_ref, cbuf_ref, *, seq, chunk):
    del layer_ref
    j = pl.program_id(1)
    n_chunks = seq // chunk
    pad = SUBLANES

    @pl.when(j == 0)
    def _():
        x = x_ref[...]
        ms = jnp.mean(x * x, axis=-1, keepdims=True)
        h_ref[...] = (x * lax.rsqrt(ms + RMS_EPS) * nw_ref[...]).astype(BF16)
        zeros = jnp.zeros((pad, COL_TILE), F32)
        cbuf_ref[0:pad, :] = zeros
        cbuf_ref[pad + seq:pad + seq + pad, :] = zeros

        ba = _dot(h_ref[...], wba_ref[...])
        lane = lax.broadcasted_iota(jnp.int32, (seq, LANES), 1)
        rowc = lax.broadcasted_iota(jnp.int32, (seq, LANES), 0) & (chunk - 1)
        n_bd = N_DIRS * DN_HEADS
        beta = jax.nn.sigmoid(ba)
        g = -jnp.exp(alog_ref[...]) * _softplus(ba + dtb_ref[...])
        g = jnp.where((lane >= n_bd) & (lane < 2 * n_bd), g, 0.0)
        pre = g
        suf = g
        s = 1
        while s < chunk:
            pre = pre + jnp.where(rowc >= s, pltpu.roll(pre, s, axis=0), 0.0)
            suf = suf + jnp.where(rowc < chunk - s, pltpu.roll(suf, seq - s, axis=0), 0.0)
            s *= 2
        is_fwd = lane < n_bd + DN_HEADS
        gc = jnp.where(is_fwd, pre, suf)
        pre3 = pre.reshape(n_chunks, chunk, LANES)
        suf3 = suf.reshape(n_chunks, chunk, LANES)
        tot_f = jnp.broadcast_to(pre3[:, chunk - 1:chunk, :], pre3.shape).reshape(seq, LANES)
        tot_b = jnp.broadcast_to(suf3[:, 0:1, :], suf3.shape).reshape(seq, LANES)
        gl = jnp.where(is_fwd, tot_f, tot_b)
        st0 = jnp.where(lane < n_bd, beta, gc)
        st_ref[0] = st0
        st_ref[1] = gl - gc
        st_ref[2] = gl
        gct = st0.T
        for c in range(n_chunks):
            gct_ref[c] = gct[n_bd:2 * n_bd, c * chunk:(c + 1) * chunk]

    cbuf_ref[pad:pad + seq, :] = _dot(h_ref[...], wm_ref[...])

    def conv_rows(r0, w_ref, k):
        win = cbuf_ref[pl.ds(r0, chunk + 2 * pad), :]
        acc = None
        for t in range(k):
            d = t - k // 2
            rolled = win if d == 0 else pltpu.roll(win, (-d) % (chunk + 2 * pad), axis=0)
            term = w_ref[t:t + 1, :] * rolled[pad:pad + chunk]
            acc = term if acc is None else acc + term
        return acc

    def l2norm_heads(t):
        outs = []
        for hh in range(COL_TILE // DN_DK):
            th = t[:, hh * DN_DK:(hh + 1) * DN_DK]
            outs.append(th * lax.rsqrt(jnp.sum(th * th, axis=-1, keepdims=True) + 1e-6))
        return jnp.concatenate(outs, axis=1)

    def rows(c):
        return pl.multiple_of(c * chunk, chunk)

    @pl.when(j == 0)
    def _():
        def body(c, carry):
            r0 = rows(c)
            t = l2norm_heads(_silu(conv_rows(r0, dnc_ref, DN_CONV)))
            main_ref[pl.ds(r0, chunk), :] = t * (DN_DK ** -0.5)
            return carry
        lax.fori_loop(0, n_chunks, body, 0)

    @pl.when(j == 1)
    def _():
        def body(c, carry):
            r0 = rows(c)
            main_ref[pl.ds(r0, chunk), :] = l2norm_heads(_silu(conv_rows(r0, dnc_ref, DN_CONV)))
            return carry
        lax.fori_loop(0, n_chunks, body, 0)

    @pl.when(j == 2)
    def _():
        def body(c, carry):
            r0 = rows(c)
            main_ref[pl.ds(r0, chunk), :] = _silu(conv_rows(r0, dnc_ref, DN_CONV))
            return carry
        lax.fori_loop(0, n_chunks, body, 0)

    @pl.when(j == 3)
    def _():
        def body(c, carry):
            r0 = rows(c)
            main_ref[pl.ds(r0, chunk), :] = _silu(cbuf_ref[pl.ds(r0 + pad, chunk), :])
            return carry
        lax.fori_loop(0, n_chunks, body, 0)

    @pl.when(j >= HY_COL0)
    def _():
        def body(c, carry):
            r0 = rows(c)
            main_ref[pl.ds(r0, chunk), :] = conv_rows(r0, hyc_ref, HY_SHORT) + hyb_ref[...]
            return carry
        lax.fori_loop(0, n_chunks, body, 0)


def _mixin(layer, x, norm_w, w_main, w_ba, dn_conv, hy_conv, hy_bias, a_log, dt_bias):
    b, seq, _ = x.shape
    chunk = DELTA_CHUNK
    n_chunks = seq // chunk
    n_bd = N_DIRS * DN_HEADS
    dn_tiles = (2 * DN_KEY + DN_VAL) // COL_TILE
    hy_tiles = 3 * HY_WIDTH // COL_TILE
    grid_spec = pltpu.PrefetchScalarGridSpec(
        num_scalar_prefetch=1,
        grid=(b, N_COL_TILES),
        in_specs=[
            pl.BlockSpec((None, seq, D_MODEL), lambda i, j, l: (i, 0, 0)),
            pl.BlockSpec((None, 1, D_MODEL), lambda i, j, l: (l[0], 0, 0)),
            pl.BlockSpec((None, D_MODEL, COL_TILE), lambda i, j, l: (l[0], 0, j)),
            pl.BlockSpec((None, D_MODEL, LANES), lambda i, j, l: (l[0], 0, 0)),
            pl.BlockSpec((None, DN_CONV, COL_TILE),
                         lambda i, j, l: (l[0], 0, jnp.minimum(j, dn_tiles - 1))),
            pl.BlockSpec((None, HY_SHORT, COL_TILE),
                         lambda i, j, l: (l[0], 0, jnp.clip(j - HY_COL0, 0, hy_tiles - 1))),
            pl.BlockSpec((None, 1, COL_TILE),
                         lambda i, j, l: (l[0], 0, jnp.clip(j - HY_COL0, 0, hy_tiles - 1))),
            pl.BlockSpec((None, 1, LANES), lambda i, j, l: (l[0], 0, 0)),
            pl.BlockSpec((None, 1, LANES), lambda i, j, l: (l[0], 0, 0)),
        ],
        out_specs=[
            pl.BlockSpec((None, seq, COL_TILE), lambda i, j, l: (i, 0, j)),
            pl.BlockSpec((None, 3, seq, LANES), lambda i, j, l: (i, 0, 0, 0)),
            pl.BlockSpec((None, n_chunks, n_bd, chunk), lambda i, j, l: (i, 0, 0, 0)),
        ],
        scratch_shapes=[pltpu.VMEM((seq, D_MODEL), BF16),
                        pltpu.VMEM((seq + 2 * SUBLANES, COL_TILE), F32)],
    )
    return pl.pallas_call(
        functools.partial(_mixin_body, seq=seq, chunk=chunk), grid_spec=grid_spec,
        out_shape=[jax.ShapeDtypeStruct((b, seq, N_MAIN), F32),
                   jax.ShapeDtypeStruct((b, 3, seq, LANES), F32),
                   jax.ShapeDtypeStruct((b, n_chunks, n_bd, chunk), F32)],
        compiler_params=_params(2), name="mixin",
    )(layer, x, norm_w, w_main, w_ba, dn_conv, hy_conv, hy_bias, a_log, dt_bias)


_INV_LEVELS = (8, 16, 32, 64, 128)


def _delta_body(layer_ref, q_ref, k_ref, v_ref, z_ref, st_ref, gct_ref, nw_ref, o_ref,
                wq_s, u_s, att_s, kdt_s, gam_s, s_s, oacc_s, lm_s, *, seq):
    del layer_ref
    hd = pl.program_id(1)
    c_ = DELTA_CHUNK
    n_chunks = seq // c_
    n_bd = N_DIRS * DN_HEADS
    row = lax.broadcasted_iota(jnp.int32, (c_, c_), 0)
    col = lax.broadcasted_iota(jnp.int32, (c_, c_), 1)
    lane = lax.broadcasted_iota(jnp.int32, (c_, LANES), 1)

    for d in range(N_DIRS):
        for li, s in enumerate(_INV_LEVELS):
            sh = s.bit_length() - 1
            rb, cb = row >> sh, col >> sh
            if d == 0:
                m = ((rb - cb) | ((cb & 1) << 1)) == 1
            else:
                m = ((cb - rb) | ((rb & 1) << 1)) == 1
            lm_s[d * len(_INV_LEVELS) + li] = jnp.where(m, 1.0, 0.0).astype(BF16)

    def pick(x, idx):
        return jnp.sum(jnp.where(lane == idx, x, 0.0), axis=1, keepdims=True)

    def phase_a(g, carry):
        r0 = pl.multiple_of(g * c_, c_)
        q = q_ref[pl.ds(r0, c_), :]
        k = k_ref[pl.ds(r0, c_), :]
        v = v_ref[pl.ds(r0, c_), :]
        st0 = st_ref[0, pl.ds(r0, c_), :]
        st1 = st_ref[1, pl.ds(r0, c_), :]
        st2 = st_ref[2, pl.ds(r0, c_), :]
        k_bf = k.astype(BF16)
        q_bf = q.astype(BF16)
        for d in range(N_DIRS):
            step = g if d == 0 else n_chunks - 1 - g
            bi = d * DN_HEADS + hd
            beta = pick(st0, bi)
            gc = pick(st0, n_bd + bi)
            egc = jnp.exp(gc)
            ekd = jnp.exp(pick(st1, n_bd + bi))
            egl = jnp.exp(pick(st2, n_bd + bi))
            gcr = gct_ref[g, pl.ds(bi, 1), :]
            incl = (row >= col) if d == 0 else (row <= col)
            strict = (row > col) if d == 0 else (row < col)
            dec = jnp.where(incl, jnp.exp(jnp.where(incl, gc - gcr, 0.0)), 0.0)
            kb = k * beta
            lhs = jnp.concatenate([kb.astype(BF16), q_bf], axis=0)
            s1 = lax.dot_general(lhs, k_bf, (((1,), (1,)), ((), ())),
                                 preferred_element_type=F32)
            a = jnp.where(strict, s1[:c_] * dec, 0.0)
            att_s[step, :, d * c_:(d + 1) * c_] = (s1[c_:] * dec).astype(BF16)
            a_bf = a.astype(BF16)

            b0 = jnp.where((row >> 3) == (col >> 3), -a, 0.0)
            b0_bf = b0.astype(BF16)
            r1 = _dot(b0_bf, b0_bf)
            r1_bf = r1.astype(BF16)
            m = _dot(jnp.concatenate([b0_bf, r1_bf], axis=0), r1_bf)
            q2 = b0 + r1 + m[:c_]
            r2 = m[c_:]
            q3 = q2 + r2 + _dot(q2.astype(BF16), r2.astype(BF16))
            x = jnp.where(row == col, 1.0, q3).astype(BF16)
            for li in range(len(_INV_LEVELS)):
                e = a_bf * lm_s[d * len(_INV_LEVELS) + li]
                m1 = _dot(e, x)
                m2 = _dot(x, (-m1).astype(BF16))
                x = x + m2.astype(BF16)

            rhs = jnp.concatenate([(v * beta).astype(BF16), (kb * egc).astype(BF16)], axis=1)
            uw = _dot(x, rhs)
            lanes_d = slice(d * DN_DV, (d + 1) * DN_DV)
            u_s[step, :, lanes_d] = uw[:, :DN_DV]
            wq_s[step, 0:c_, lanes_d] = uw[:, DN_DV:].astype(BF16)
            wq_s[step, c_:2 * c_, lanes_d] = (q * egc).astype(BF16)
            kdt_s[step, d * DN_DK:(d + 1) * DN_DK, :] = (k * ekd).T.astype(BF16)
            gam_s[step, d * DN_DK:(d + 1) * DN_DK, :] = jnp.broadcast_to(
                egl[0:DN_DK, :], (DN_DK, N_DIRS * DN_DV))
        return carry

    lax.fori_loop(0, n_chunks, phase_a, 0)

    s_s[...] = jnp.zeros_like(s_s)
    oacc_s[...] = jnp.zeros_like(oacc_s)
    w2 = N_DIRS * DN_DV
    lane2 = lax.broadcasted_iota(jnp.int32, (c_, w2), 1)
    blockdiag = (lax.broadcasted_iota(jnp.int32, (w2, w2), 0) >> 7) == (
        lax.broadcasted_iota(jnp.int32, (w2, w2), 1) >> 7)

    def scan_step(s, carry):
        s_bf = s_s[...].astype(BF16)
        wqs = _dot(wq_s[s], s_bf)
        vn = u_s[s] - wqs[:c_]
        vn_bf = vn.astype(BF16)
        rhs = jnp.concatenate([jnp.where(lane2 < DN_DV, vn, 0.0).astype(BF16),
                               jnp.where(lane2 >= DN_DV, vn, 0.0).astype(BF16)], axis=0)
        o = wqs[c_:] + _dot(att_s[s], rhs)
        kv = _dot(kdt_s[s], vn_bf)
        s_s[...] = s_s[...] * gam_s[s] + jnp.where(blockdiag, kv, 0.0)
        rf = pl.multiple_of(s * c_, c_)
        rb = pl.multiple_of((n_chunks - 1 - s) * c_, c_)
        oacc_s[pl.ds(rf, c_), :] += o[:, :DN_DV]
        oacc_s[pl.ds(rb, c_), :] += o[:, DN_DV:]
        return carry

    lax.fori_loop(0, n_chunks, scan_step, 0)

    o = oacc_s[...]
    ms = jnp.mean(o * o, axis=-1, keepdims=True)
    o_ref[...] = o * lax.rsqrt(ms + RMS_EPS) * nw_ref[...] * z_ref[...]


def _delta(layer, main, st, gct, norm_w):
    b, seq, _ = main.shape
    c_ = DELTA_CHUNK
    n_chunks = seq // c_
    n_bd = N_DIRS * DN_HEADS
    n_lvl = len(_INV_LEVELS)
    w2 = N_DIRS * DN_DV
    kcol = DN_KEY // DN_DK
    head_spec = lambda off: pl.BlockSpec((None, seq, DN_DK), lambda i, h, l: (i, 0, off + h))
    grid_spec = pltpu.PrefetchScalarGridSpec(
        num_scalar_prefetch=1,
        grid=(b, DN_HEADS),
        in_specs=[
            head_spec(0), head_spec(kcol), head_spec(2 * kcol), head_spec(3 * kcol),
            pl.BlockSpec((None, 3, seq, LANES), lambda i, h, l: (i, 0, 0, 0)),
            pl.BlockSpec((None, n_chunks, n_bd, c_), lambda i, h, l: (i, 0, 0, 0)),
            pl.BlockSpec((None, 1, DN_DV), lambda i, h, l: (l[0], 0, 0)),
        ],
        out_specs=pl.BlockSpec((None, seq, DN_DV), lambda i, h, l: (i, 0, h)),
        scratch_shapes=[
            pltpu.VMEM((n_chunks, 2 * c_, w2), BF16),
            pltpu.VMEM((n_chunks, c_, w2), F32),
            pltpu.VMEM((n_chunks, c_, N_DIRS * c_), BF16),
            pltpu.VMEM((n_chunks, N_DIRS * DN_DK, c_), BF16),
            pltpu.VMEM((n_chunks, N_DIRS * DN_DK, w2), F32),
            pltpu.VMEM((N_DIRS * DN_DK, w2), F32),
            pltpu.VMEM((seq, DN_DV), F32),
            pltpu.VMEM((N_DIRS * n_lvl, c_, c_), BF16),
        ],
    )
    return pl.pallas_call(
        functools.partial(_delta_body, seq=seq), grid_spec=grid_spec,
        out_shape=jax.ShapeDtypeStruct((b, seq, DN_VAL), F32),
        compiler_params=_params(2), name="delta",
    )(layer, main, main, main, main, st, gct, norm_w)


def _dft_tables(seq):
    n = 2 * seq
    k = lax.broadcasted_iota(jnp.int32, (seq, seq), 0)
    t = lax.broadcasted_iota(jnp.int32, (seq, seq), 1)
    ang = ((k * t) & (n - 1)).astype(F32) * (2.0 * math.pi / n)
    return jnp.cos(ang).astype(BF16), jnp.sin(ang).astype(BF16)


def _filter_feats(seq):
    t = jnp.linspace(0.0, 1.0, seq, dtype=F32)[:, None]
    bands = (HY_EMB - 1) // 2
    ang = ((2.0 * math.pi / seq) * jnp.arange(seq, dtype=F32)[:, None]
           * jnp.linspace(1e-4, bands - 1, bands, dtype=F32)[None, :])
    feats = jnp.concatenate([t, jnp.cos(ang), -jnp.sin(ang)], axis=-1)
    return jnp.pad(feats, ((0, 0), (0, PAD_HID - HY_EMB)))


def _filter_deltas():
    max_decay = math.log(HY_DECAY_TARGET) / HY_FAST_DECAY_PCT
    min_decay = math.log(HY_DECAY_TARGET) / HY_SLOW_DECAY_PCT
    return jnp.abs(jnp.linspace(min_decay, max_decay, HY_WIDTH, dtype=F32))[None, :]


def _filt_body(layer_ref, feats_ref, w1_ref, b1_ref, w2_ref, b2_ref, w3_ref, b3_ref, fr_ref,
               wof_ref, wob_ref, dl_ref, c_ref, s_ref, kre_ref, kim_ref, hdn_ref, *, seq):
    del layer_ref
    n = 2 * seq

    @pl.when(pl.program_id(0) == 0)
    def _():
        fr = fr_ref[...]
        hp = functools.partial(jnp.dot, preferred_element_type=F32, precision=HIGHEST)
        h = jnp.sin(fr * (hp(feats_ref[...], w1_ref[...]) + b1_ref[...]))
        h = jnp.sin(fr * (hp(h, w2_ref[...]) + b2_ref[...]))
        hdn_ref[...] = jnp.sin(fr * (hp(h, w3_ref[...]) + b3_ref[...]))

    rowi = lax.broadcasted_iota(jnp.int32, (seq, HY_CT), 0)
    tt = rowi.astype(F32) * (1.0 / (seq - 1))
    window = jnp.exp(-tt * dl_ref[...])
    hdn = hdn_ref[...]
    fwd = jnp.dot(hdn, wof_ref[...], preferred_element_type=F32, precision=HIGHEST) * window
    bwd = jnp.dot(hdn, wob_ref[...], preferred_element_type=F32, precision=HIGHEST) * window
    bsh = jnp.where(rowi >= 1, pltpu.roll(bwd, 1, axis=0), 0.0)
    rhs = jnp.concatenate([fwd, bsh], axis=1).astype(BF16)
    gc = _dot(c_ref[...], rhs)
    gs = _dot(s_ref[...], rhs)
    alt = (1 - 2 * (rowi & 1)).astype(F32)
    nyq = jnp.sum(alt * (fwd + bsh), axis=0, keepdims=True)
    kre = gc[:, :HY_CT] + gc[:, HY_CT:]
    kim = gs[:, HY_CT:] - gs[:, :HY_CT]
    first = rowi == 0
    kre_ref[...] = kre * jnp.where(first, 1.0 / n, 2.0 / n)
    kim_ref[...] = jnp.where(first, nyq * (1.0 / n), kim * (2.0 / n))


def _filter_spectrum(layer, feats, w1, b1, w2, b2, w3, b3, freq, wout, deltas, ctab, stab):
    seq = feats.shape[0]
    n_oc = HY_ORDER * HY_WIDTH
    tiles = n_oc // HY_CT
    per_w = HY_WIDTH // HY_CT
    sq = lambda: pl.BlockSpec((None, PAD_HID, PAD_HID), lambda j, l: (l[0], 0, 0))
    vec = lambda: pl.BlockSpec((None, 1, PAD_HID), lambda j, l: (l[0], 0, 0))
    grid_spec = pltpu.PrefetchScalarGridSpec(
        num_scalar_prefetch=1,
        grid=(tiles,),
        in_specs=[
            pl.BlockSpec((seq, PAD_HID), lambda j, l: (0, 0)),
            sq(), vec(), sq(), vec(), sq(), vec(), vec(),
            pl.BlockSpec((None, PAD_HID, HY_CT), lambda j, l: (l[0], 0, j)),
            pl.BlockSpec((None, PAD_HID, HY_CT), lambda j, l: (l[0], 0, tiles + j)),
            pl.BlockSpec((1, HY_CT), lambda j, l: (0, j % per_w)),
            _resident((seq, seq), lambda j, l: (0, 0)),
            _resident((seq, seq), lambda j, l: (0, 0)),
        ],
        out_specs=[pl.BlockSpec((None, seq, HY_CT), lambda j, l: (j % per_w, 0, j // per_w)),
                   pl.BlockSpec((None, seq, HY_CT), lambda j, l: (j % per_w, 0, j // per_w))],
        scratch_shapes=[pltpu.VMEM((seq, PAD_HID), F32)],
    )
    return pl.pallas_call(
        functools.partial(_filt_body, seq=seq), grid_spec=grid_spec,
        out_shape=[jax.ShapeDtypeStruct((per_w, seq, HY_ORDER * HY_CT), F32)] * 2,
        compiler_params=_params(1), name="hyena_filter",
    )(layer, feats, w1, b1, w2, b2, w3, b3, freq, wout, wout, deltas, ctab, stab)


def _hyena_body(layer_ref, v_ref, x1_ref, x2_ref, kre_ref, kim_ref, skip_ref, c_ref, s_ref, o_ref,
                z_s, zb_s, xc_s, xs_s, ya_s, yb_s, *, seq):
    del layer_ref
    rc = HY_ROWS
    n_rc = seq // rc
    alt = (1 - 2 * (lax.broadcasted_iota(jnp.int32, (rc, HY_CT), 0) & 1)).astype(F32)

    def rows(c):
        return pl.ds(pl.multiple_of(c * rc, rc), rc)

    for o, gate_ref in enumerate((x1_ref, x2_ref)):
        z_ref = v_ref if o == 0 else z_s
        last = o == HY_ORDER - 1
        cols = slice(o * HY_CT, (o + 1) * HY_CT)

        def to_bf16(c, acc):
            z = z_ref[rows(c), :]
            zb_s[rows(c), :] = z.astype(BF16)
            return acc + jnp.sum(alt * z, axis=0, keepdims=True)

        xnyq = lax.fori_loop(0, n_rc, to_bf16, jnp.zeros((1, HY_CT), F32))
        xc_s[...] = _dot(c_ref[...], zb_s[...])
        xs_s[...] = _dot(s_ref[...], zb_s[...])

        def spectrum(c, carry):
            xc, xs = xc_s[rows(c), :], xs_s[rows(c), :]
            kre, kim = kre_ref[rows(c), cols], kim_ref[rows(c), cols]
            ya_s[rows(c), :] = (xc * kre + xs * kim).astype(BF16)
            yb_s[rows(c), :] = (xs * kre - xc * kim).astype(BF16)
            return carry

        lax.fori_loop(0, n_rc, spectrum, 0)
        xc_s[...] = _dot(c_ref[...], ya_s[...]) + _dot(s_ref[...], yb_s[...])
        ynyq = xnyq * kim_ref[0:1, cols]
        skip = skip_ref[o:o + 1, :]

        def gate(c, carry):
            z = z_ref[rows(c), :]
            z_new = gate_ref[rows(c), :] * (xc_s[rows(c), :] + alt * ynyq + skip * z)
            (o_ref if last else z_s)[rows(c), :] = z_new
            return carry

        lax.fori_loop(0, n_rc, gate, 0)


def _hyena(layer, main, kre, kim, skip, ctab, stab):
    b, seq, _ = main.shape
    tiles = HY_WIDTH // HY_CT
    col0 = (2 * DN_KEY + 2 * DN_VAL) // HY_CT
    hy_spec = lambda part: pl.BlockSpec(
        (None, seq, HY_CT), lambda j, i, l: (i, 0, col0 + part * tiles + j))
    k_spec = lambda: pl.BlockSpec((None, seq, HY_ORDER * HY_CT), lambda j, i, l: (j, 0, 0),
                                  pipeline_mode=pl.Buffered(1))
    grid_spec = pltpu.PrefetchScalarGridSpec(
        num_scalar_prefetch=1,
        grid=(tiles, b),
        in_specs=[
            hy_spec(0), hy_spec(1), hy_spec(2), k_spec(), k_spec(),
            pl.BlockSpec((None, None, HY_ORDER, HY_CT), lambda j, i, l: (l[0], j, 0, 0)),
            _resident((seq, seq), lambda j, i, l: (0, 0)),
            _resident((seq, seq), lambda j, i, l: (0, 0)),
        ],
        out_specs=pl.BlockSpec((None, seq, HY_CT), lambda j, i, l: (i, 0, j)),
        scratch_shapes=[pltpu.VMEM((seq, HY_CT), F32), pltpu.VMEM((seq, HY_CT), BF16),
                        pltpu.VMEM((seq, HY_CT), F32), pltpu.VMEM((seq, HY_CT), F32),
                        pltpu.VMEM((seq, HY_CT), BF16), pltpu.VMEM((seq, HY_CT), BF16)],
    )
    return pl.pallas_call(
        functools.partial(_hyena_body, seq=seq), grid_spec=grid_spec,
        out_shape=jax.ShapeDtypeStruct((b, seq, HY_WIDTH), F32),
        compiler_params=_params(2), name="hyena",
    )(layer, main, main, main, kre, kim, skip, ctab, stab)


def _mixout_body(layer_ref, x_ref, odn_ref, zz_ref, hn_ref, wo_ref, o_ref):
    del layer_ref
    zz = zz_ref[...]
    ms = jnp.mean(zz * zz, axis=-1, keepdims=True)
    zn = zz * lax.rsqrt(ms + RMS_EPS) * hn_ref[...]
    y = _dot(odn_ref[...].astype(BF16), wo_ref[0:DN_VAL, :])
    y = y + _dot(zn.astype(BF16), wo_ref[DN_VAL:D_MODEL, :])
    o_ref[...] = x_ref[...] + y


def _mixout(layer, x2d, odn2d, zz2d, hy_norm, w_out, *, tm):
    m = x2d.shape[0]
    grid_spec = pltpu.PrefetchScalarGridSpec(
        num_scalar_prefetch=1,
        grid=(m // tm,),
        in_specs=[
            pl.BlockSpec((tm, D_MODEL), lambda i, l: (i, 0)),
            pl.BlockSpec((tm, DN_VAL), lambda i, l: (i, 0)),
            pl.BlockSpec((tm, HY_WIDTH), lambda i, l: (i, 0)),
            pl.BlockSpec((None, 1, HY_WIDTH), lambda i, l: (l[0], 0, 0)),
            pl.BlockSpec((None, D_MODEL, D_MODEL), lambda i, l: (l[0], 0, 0)),
        ],
        out_specs=pl.BlockSpec((tm, D_MODEL), lambda i, l: (i, 0)),
    )
    return pl.pallas_call(
        _mixout_body, grid_spec=grid_spec,
        out_shape=jax.ShapeDtypeStruct((m, D_MODEL), F32),
        compiler_params=_params(1), name="mixout",
    )(layer, x2d, odn2d, zz2d, hy_norm, w_out)


def _final_norm_body(x_ref, w_ref, o_ref):
    x = x_ref[...]
    ms = jnp.mean(x * x, axis=-1, keepdims=True)
    o_ref[...] = x * lax.rsqrt(ms + RMS_EPS) * w_ref[...]


def _final_norm(x2d, w, *, tm):
    m = x2d.shape[0]
    return pl.pallas_call(
        _final_norm_body, grid=(m // tm,),
        in_specs=[pl.BlockSpec((tm, D_MODEL), lambda i: (i, 0)),
                  pl.BlockSpec((1, D_MODEL), lambda i: (0, 0))],
        out_specs=pl.BlockSpec((tm, D_MODEL), lambda i: (i, 0)),
        out_shape=jax.ShapeDtypeStruct((m, D_MODEL), F32),
        compiler_params=_params(1), name="final_norm",
    )(x2d, w)


def _pad_to(a, axis, size):
    pads = [(0, 0)] * a.ndim
    pads[axis] = (0, size - a.shape[axis])
    return jnp.pad(a, pads)


def _row_tile(m):
    return 1024 if m % 1024 == 0 else m


def kernel(x, ffn1_norm, ffn1_w_gate, ffn1_w_up, ffn1_w_down, mix_norm, w_in, dn_conv, dn_a_log, dn_dt_bias, dn_norm, hy_conv, hy_conv_bias, hy_f_w1, hy_f_b1, hy_f_w2, hy_f_b2, hy_f_w3, hy_f_b3, hy_f_freq, hy_f_wout, hy_skip, hy_norm, w_out, ffn2_norm, ffn2_w_gate, ffn2_w_up, ffn2_w_down, final_norm):
    b, seq, _ = x.shape
    m = b * seq
    tm = _row_tile(m)
    n_bd = N_DIRS * DN_HEADS
    bf = lambda a: a.astype(BF16)
    row3 = lambda a: a[:, None, :]

    ba0 = 2 * DN_KEY + 2 * DN_VAL
    w_main = bf(jnp.concatenate([w_in[:, :, :ba0], w_in[:, :, ba0 + 2 * n_bd:]], axis=-1))
    w_ba = bf(_pad_to(w_in[:, :, ba0:ba0 + 2 * n_bd], 2, LANES))
    lane_params = lambda p: _pad_to(
        jnp.pad(p.reshape(DEPTH, 1, n_bd), ((0, 0), (0, 0), (n_bd, 0))), 2, LANES)
    a_log = lane_params(dn_a_log)
    dt_bias = lane_params(dn_dt_bias)
    f_w1 = _pad_to(_pad_to(hy_f_w1, 1, PAD_HID), 2, PAD_HID)
    f_w2 = _pad_to(_pad_to(hy_f_w2, 1, PAD_HID), 2, PAD_HID)
    f_w3 = _pad_to(_pad_to(hy_f_w3, 1, PAD_HID), 2, PAD_HID)
    f_b1, f_b2, f_b3, f_fr = (row3(_pad_to(p, 1, PAD_HID))
                              for p in (hy_f_b1, hy_f_b2, hy_f_b3, hy_f_freq))
    f_wout = _pad_to(hy_f_wout, 1, PAD_HID)
    tiles = HY_WIDTH // HY_CT
    skip = hy_skip.reshape(DEPTH, HY_ORDER, tiles, HY_CT).transpose(0, 2, 1, 3)
    weights = dict(
        ffn1=(row3(ffn1_norm), bf(ffn1_w_gate), bf(ffn1_w_up), bf(ffn1_w_down)),
        ffn2=(row3(ffn2_norm), bf(ffn2_w_gate), bf(ffn2_w_up), bf(ffn2_w_down)),
        w_out=bf(w_out))

    ctab, stab = _dft_tables(seq)
    feats = _filter_feats(seq)
    deltas = _filter_deltas()

    def layer_fn(i, xc):
        layer = jnp.full((1,), i, jnp.int32)
        x2 = _ffn(layer, xc.reshape(m, D_MODEL), *weights["ffn1"], tm=tm)
        x3 = x2.reshape(b, seq, D_MODEL)
        main, st, gct = _mixin(layer, x3, row3(mix_norm), w_main, w_ba, dn_conv, hy_conv,
                               row3(hy_conv_bias), a_log, dt_bias)
        kre, kim = _filter_spectrum(layer, feats, f_w1, f_b1, f_w2, f_b2, f_w3, f_b3, f_fr,
                                    f_wout, deltas, ctab, stab)
        o_dn = _delta(layer, main, st, gct, row3(dn_norm))
        zz = _hyena(layer, main, kre, kim, skip, ctab, stab)
        x4 = _mixout(layer, x2, o_dn.reshape(m, DN_VAL), zz.reshape(m, HY_WIDTH),
                     row3(hy_norm), weights["w_out"], tm=tm)
        x5 = _ffn(layer, x4, *weights["ffn2"], tm=tm)
        return x5.reshape(b, seq, D_MODEL)

    xf = lax.fori_loop(0, DEPTH, layer_fn, x)
    return _final_norm(xf.reshape(m, D_MODEL), final_norm[None, :], tm=tm).reshape(b, seq, D_MODEL)
```

```python
import functools
import math
import types

import jax
import jax.numpy as jnp
from jax import lax
from jax.experimental import pallas as pl
from jax.experimental.pallas import tpu as pltpu

F32 = jnp.float32
BF16 = jnp.bfloat16
HIGHEST = lax.Precision.HIGHEST

D_MODEL = 1024
DEPTH = 4
N_DIRS = 2
DN_HEADS = 4
DN_DK = 128
DN_DV = 128
DN_KEY = DN_HEADS * DN_DK
DN_VAL = DN_HEADS * DN_DV
DN_CONV = 5
HY_WIDTH = D_MODEL - DN_VAL
HY_ORDER = 2
HY_SHORT = 3
HY_EMB = 33
HY_HID = 64
HY_DIRS = 2
HY_FAST_DECAY_PCT = 0.3
HY_SLOW_DECAY_PCT = 1.5
HY_DECAY_TARGET = 1e-2
D_FF = 2816
RMS_EPS = 1e-6

LANES = 128
SUBLANES = 8
MXU_DIM = 256
VMEM_LIMIT_BYTES = 58 * 1024 * 1024

N_MAIN = 2 * DN_KEY + 2 * DN_VAL + 3 * HY_WIDTH
COL_TILE = 512
N_COL_TILES = N_MAIN // COL_TILE
HY_COL0 = (2 * DN_KEY + 2 * DN_VAL) // COL_TILE
FF_TILE = MXU_DIM
N_FF_TILES = D_FF // FF_TILE
DELTA_CHUNK = MXU_DIM
PAR_CHUNKS = 4
HY_CT = MXU_DIM
HY_ROWS = 256
PAD_HID = LANES


def _params(n_grid):
    return pltpu.CompilerParams(
        dimension_semantics=("arbitrary",) * n_grid,
        vmem_limit_bytes=VMEM_LIMIT_BYTES)


def _resident(block, imap):
    return pl.BlockSpec(block, imap, pipeline_mode=pl.Buffered(1))


def _dot(a, b):
    return jnp.dot(a, b, preferred_element_type=F32)


def _silu(x):
    return x * jax.nn.sigmoid(x)


def _ffn_body(layer_ref, x_ref, nw_ref, wg_ref, wu_ref, wd_ref, o_ref, h_ref, a_ref):
    del layer_ref
    x = x_ref[...]
    ms = jnp.mean(x * x, axis=-1, keepdims=True)
    h_ref[...] = (x * lax.rsqrt(ms + RMS_EPS) * nw_ref[...]).astype(BF16)
    for f in range(N_FF_TILES):
        cols = slice(f * FF_TILE, (f + 1) * FF_TILE)
        g = _dot(h_ref[...], wg_ref[:, cols])
        u = _dot(h_ref[...], wu_ref[:, cols])
        a_ref[:, cols] = (_silu(g) * u).astype(BF16)
    o_ref[...] = x_ref[...] + 0.5 * _dot(a_ref[...], wd_ref[...])


def _ffn(layer, x2d, norm_w, wg, wu, wd, *, tm):
    m = x2d.shape[0]
    grid_spec = pltpu.PrefetchScalarGridSpec(
        num_scalar_prefetch=1,
        grid=(m // tm,),
        in_specs=[
            pl.BlockSpec((tm, D_MODEL), lambda i, l: (i, 0)),
            pl.BlockSpec((None, 1, D_MODEL), lambda i, l: (l[0], 0, 0)),
            _resident((None, D_MODEL, D_FF), lambda i, l: (l[0], 0, 0)),
            _resident((None, D_MODEL, D_FF), lambda i, l: (l[0], 0, 0)),
            _resident((None, D_FF, D_MODEL), lambda i, l: (l[0], 0, 0)),
        ],
        out_specs=pl.BlockSpec((tm, D_MODEL), lambda i, l: (i, 0)),
        scratch_shapes=[pltpu.VMEM((tm, D_MODEL), BF16), pltpu.VMEM((tm, D_FF), BF16)],
    )
    return pl.pallas_call(
        _ffn_body, grid_spec=grid_spec,
        out_shape=jax.ShapeDtypeStruct((m, D_MODEL), F32),
        compiler_params=_params(1), name="ffn",
    )(layer, x2d, norm_w, wg, wu, wd)


def _softplus(x):
    return jnp.maximum(x, 0.0) + jnp.log1p(jnp.exp(-jnp.abs(x)))


def _mixin_body(layer_ref, x_ref, nw_ref, wm_ref, wba_ref, dnc_ref, hyc_ref, hyb_ref,
                alog_ref, dtb_ref, main_ref, st_ref, gct_ref, h_ref, cbuf_ref, *, seq, chunk):
    del layer_ref
    j = pl.program_id(1)
    n_chunks = seq // chunk
    pad = SUBLANES

    @pl.when(j == 0)
    def _():
        x = x_ref[...]
        ms = jnp.mean(x * x, axis=-1, keepdims=True)
        h_ref[...] = (x * lax.rsqrt(ms + RMS_EPS) * nw_ref[...]).astype(BF16)
        zeros = jnp.zeros((pad, COL_TILE), F32)
        cbuf_ref[0:pad, :] = zeros
        cbuf_ref[pad + seq:pad + seq + pad, :] = zeros

        ba = _dot(h_ref[...], wba_ref[...])
        lane = lax.broadcasted_iota(jnp.int32, (seq, LANES), 1)
        rowc = lax.broadcasted_iota(jnp.int32, (seq, LANES), 0) & (chunk - 1)
        n_bd = N_DIRS * DN_HEADS
        beta = jax.nn.sigmoid(ba)
        g = -jnp.exp(alog_ref[...]) * _softplus(ba + dtb_ref[...])
        g = jnp.where((lane >= n_bd) & (lane < 2 * n_bd), g, 0.0)
        pre = g
        suf = g
        s = 1
        while s < chunk:
            pre = pre + jnp.where(rowc >= s, pltpu.roll(pre, s, axis=0), 0.0)
            suf = suf + jnp.where(rowc < chunk - s, pltpu.roll(suf, seq - s, axis=0), 0.0)
            s *= 2
        is_fwd = lane < n_bd + DN_HEADS
        gc = jnp.where(is_fwd, pre, suf)
        pre3 = pre.reshape(n_chunks, chunk, LANES)
        suf3 = suf.reshape(n_chunks, chunk, LANES)
        tot_f = jnp.broadcast_to(pre3[:, chunk - 1:chunk, :], pre3.shape).reshape(seq, LANES)
        tot_b = jnp.broadcast_to(suf3[:, 0:1, :], suf3.shape).reshape(seq, LANES)
        gl = jnp.where(is_fwd, tot_f, tot_b)
        st0 = jnp.where(lane < n_bd, beta, gc)
        st_ref[0] = st0
        st_ref[1] = gl - gc
        st_ref[2] = gl
        gct = st0.T
        for c in range(n_chunks):
            gct_ref[c] = gct[n_bd:2 * n_bd, c * chunk:(c + 1) * chunk]

    cbuf_ref[pad:pad + seq, :] = _dot(h_ref[...], wm_ref[...])

    def conv_rows(r0, w_ref, k):
        win = cbuf_ref[pl.ds(r0, chunk + 2 * pad), :]
        acc = None
        for t in range(k):
            d = t - k // 2
            rolled = win if d == 0 else pltpu.roll(win, (-d) % (chunk + 2 * pad), axis=0)
            term = w_ref[t:t + 1, :] * rolled[pad:pad + chunk]
            acc = term if acc is None else acc + term
        return acc

    def l2norm_heads(t):
        outs = []
        for hh in range(COL_TILE // DN_DK):
            th = t[:, hh * DN_DK:(hh + 1) * DN_DK]
            outs.append(th * lax.rsqrt(jnp.sum(th * th, axis=-1, keepdims=True) + 1e-6))
        return jnp.concatenate(outs, axis=1)

    def rows(c):
        return pl.multiple_of(c * chunk, chunk)

    @pl.when(j == 0)
    def _():
        def body(c, carry):
            r0 = rows(c)
            t = l2norm_heads(_silu(conv_rows(r0, dnc_ref, DN_CONV)))
            main_ref[pl.ds(r0, chunk), :] = t * (DN_DK ** -0.5)
            return carry
        lax.fori_loop(0, n_chunks, body, 0)

    @pl.when(j == 1)
    def _():
        def body(c, carry):
            r0 = rows(c)
            main_ref[pl.ds(r0, chunk), :] = l2norm_heads(_silu(conv_rows(r0, dnc_ref, DN_CONV)))
            return carry
        lax.fori_loop(0, n_chunks, body, 0)

    @pl.when(j == 2)
    def _():
        def body(c, carry):
            r0 = rows(c)
            main_ref[pl.ds(r0, chunk), :] = _silu(conv_rows(r0, dnc_ref, DN_CONV))
            return carry
        lax.fori_loop(0, n_chunks, body, 0)

    @pl.when(j == 3)
    def _():
        def body(c, carry):
            r0 = rows(c)
            main_ref[pl.ds(r0, chunk), :] = _silu(cbuf_ref[pl.ds(r0 + pad, chunk), :])
            return carry
        lax.fori_loop(0, n_chunks, body, 0)

    @pl.when(j >= HY_COL0)
    def _():
        def body(c, carry):
            r0 = rows(c)
            main_ref[pl.ds(r0, chunk), :] = conv_rows(r0, hyc_ref, HY_SHORT) + hyb_ref[...]
            return carry
        lax.fori_loop(0, n_chunks, body, 0)


def _mixin(layer, x, norm_w, w_main, w_ba, dn_conv, hy_conv, hy_bias, a_log, dt_bias):
    b, seq, _ = x.shape
    chunk = DELTA_CHUNK
    n_chunks = seq // chunk
    n_bd = N_DIRS * DN_HEADS
    dn_tiles = (2 * DN_KEY + DN_VAL) // COL_TILE
    hy_tiles = 3 * HY_WIDTH // COL_TILE
    grid_spec = pltpu.PrefetchScalarGridSpec(
        num_scalar_prefetch=1,
        grid=(b, N_COL_TILES),
        in_specs=[
            pl.BlockSpec((None, seq, D_MODEL), lambda i, j, l: (i, 0, 0)),
            pl.BlockSpec((None, 1, D_MODEL), lambda i, j, l: (l[0], 0, 0)),
            pl.BlockSpec((None, D_MODEL, COL_TILE), lambda i, j, l: (l[0], 0, j)),
            pl.BlockSpec((None, D_MODEL, LANES), lambda i, j, l: (l[0], 0, 0)),
            pl.BlockSpec((None, DN_CONV, COL_TILE),
                         lambda i, j, l: (l[0], 0, jnp.minimum(j, dn_tiles - 1))),
            pl.BlockSpec((None, HY_SHORT, COL_TILE),
                         lambda i, j, l: (l[0], 0, jnp.clip(j - HY_COL0, 0, hy_tiles - 1))),
            pl.BlockSpec((None, 1, COL_TILE),
                         lambda i, j, l: (l[0], 0, jnp.clip(j - HY_COL0, 0, hy_tiles - 1))),
            pl.BlockSpec((None, 1, LANES), lambda i, j, l: (l[0], 0, 0)),
            pl.BlockSpec((None, 1, LANES), lambda i, j, l: (l[0], 0, 0)),
        ],
        out_specs=[
            pl.BlockSpec((None, seq, COL_TILE), lambda i, j, l: (i, 0, j)),
            pl.BlockSpec((None, 3, seq, LANES), lambda i, j, l: (i, 0, 0, 0)),
            pl.BlockSpec((None, n_chunks, n_bd, chunk), lambda i, j, l: (i, 0, 0, 0)),
        ],
        scratch_shapes=[pltpu.VMEM((seq, D_MODEL), BF16),
                        pltpu.VMEM((seq + 2 * SUBLANES, COL_TILE), F32)],
    )
    return pl.pallas_call(
        functools.partial(_mixin_body, seq=seq, chunk=chunk), grid_spec=grid_spec,
        out_shape=[jax.ShapeDtypeStruct((b, seq, N_MAIN), F32),
                   jax.ShapeDtypeStruct((b, 3, seq, LANES), F32),
                   jax.ShapeDtypeStruct((b, n_chunks, n_bd, chunk), F32)],
        compiler_params=_params(2), name="mixin",
    )(layer, x, norm_w, w_main, w_ba, dn_conv, hy_conv, hy_bias, a_log, dt_bias)


_INV_LEVELS = (8, 16, 32, 64, 128)


def _delta_body(layer_ref, q_ref, k_ref, v_ref, z_ref, st_ref, gct_ref, nw_ref, o_ref,
                wq_s, u_s, att_s, kdt_s, gam_s, s_s, oacc_s, lm_s, *, seq):
    del layer_ref
    hd = pl.program_id(1)
    c_ = DELTA_CHUNK
    n_chunks = seq // c_
    n_bd = N_DIRS * DN_HEADS
    row = lax.broadcasted_iota(jnp.int32, (c_, c_), 0)
    col = lax.broadcasted_iota(jnp.int32, (c_, c_), 1)
    lane = lax.broadcasted_iota(jnp.int32, (c_, LANES), 1)

    for d in range(N_DIRS):
        for li, s in enumerate(_INV_LEVELS):
            sh = s.bit_length() - 1
            rb, cb = row >> sh, col >> sh
            if d == 0:
                m = ((rb - cb) | ((cb & 1) << 1)) == 1
            else:
                m = ((cb - rb) | ((rb & 1) << 1)) == 1
            lm_s[d * len(_INV_LEVELS) + li] = jnp.where(m, 1.0, 0.0).astype(BF16)

    def pick(x, idx):
        return jnp.sum(jnp.where(lane == idx, x, 0.0), axis=1, keepdims=True)

    n_lvl = len(_INV_LEVELS)
    par = math.gcd(PAR_CHUNKS, n_chunks)

    def phase_a(it, carry):
        chains = []
        for j in range(par):
            g = it * par + j
            r0 = pl.multiple_of(g * c_, c_)
            q = q_ref[pl.ds(r0, c_), :]
            k = k_ref[pl.ds(r0, c_), :]
            v = v_ref[pl.ds(r0, c_), :]
            st0 = st_ref[0, pl.ds(r0, c_), :]
            st1 = st_ref[1, pl.ds(r0, c_), :]
            st2 = st_ref[2, pl.ds(r0, c_), :]
            k_bf = k.astype(BF16)
            q_bf = q.astype(BF16)
            for d in range(N_DIRS):
                ch = types.SimpleNamespace(d=d, step=g if d == 0 else n_chunks - 1 - g,
                                           lanes=slice(d * DN_DV, (d + 1) * DN_DV), k_bf=k_bf)
                bi = d * DN_HEADS + hd
                beta = pick(st0, bi)
                gc = pick(st0, n_bd + bi)
                egc = jnp.exp(gc)
                ekd = jnp.exp(pick(st1, n_bd + bi))
                egl = jnp.exp(pick(st2, n_bd + bi))
                gcr = gct_ref[g, pl.ds(bi, 1), :]
                incl = (row >= col) if d == 0 else (row <= col)
                ch.strict = (row > col) if d == 0 else (row < col)
                ch.dec = jnp.where(incl, jnp.exp(jnp.where(incl, gc - gcr, 0.0)), 0.0)
                kb = k * beta
                ch.lhs = jnp.concatenate([kb.astype(BF16), q_bf], axis=0)
                ch.rhs = jnp.concatenate([(v * beta).astype(BF16), (kb * egc).astype(BF16)], axis=1)
                wq_s[ch.step, c_:2 * c_, ch.lanes] = (q * egc).astype(BF16)
                kdt_s[ch.step, d * DN_DK:(d + 1) * DN_DK, :] = (k * ekd).T.astype(BF16)
                gam_s[ch.step, d * DN_DK:(d + 1) * DN_DK, :] = jnp.broadcast_to(
                    egl[0:DN_DK, :], (DN_DK, N_DIRS * DN_DV))
                chains.append(ch)

        for ch in chains:
            ch.s1 = lax.dot_general(ch.lhs, ch.k_bf, (((1,), (1,)), ((), ())),
                                    preferred_element_type=F32)
        for ch in chains:
            a = jnp.where(ch.strict, ch.s1[:c_] * ch.dec, 0.0)
            att_s[ch.step, :, ch.d * c_:(ch.d + 1) * c_] = (ch.s1[c_:] * ch.dec).astype(BF16)
            ch.a_bf = a.astype(BF16)
            ch.b0 = jnp.where((row >> 3) == (col >> 3), -a, 0.0)
            ch.b0_bf = ch.b0.astype(BF16)
        for ch in chains:
            ch.r1 = _dot(ch.b0_bf, ch.b0_bf)
        for ch in chains:
            r1_bf = ch.r1.astype(BF16)
            ch.m = _dot(jnp.concatenate([ch.b0_bf, r1_bf], axis=0), r1_bf)
        for ch in chains:
            ch.q2 = ch.b0 + ch.r1 + ch.m[:c_]
            ch.r2 = ch.m[c_:]
            ch.q2r2 = _dot(ch.q2.astype(BF16), ch.r2.astype(BF16))
        for ch in chains:
            ch.x = jnp.where(row == col, 1.0, ch.q2 + ch.r2 + ch.q2r2).astype(BF16)
        for li in range(n_lvl):
            for ch in chains:
                ch.m1 = _dot(ch.a_bf * lm_s[ch.d * n_lvl + li], ch.x)
            for ch in chains:
                ch.m2 = _dot(ch.x, (-ch.m1).astype(BF16))
            for ch in chains:
                ch.x = ch.x + ch.m2.astype(BF16)
        for ch in chains:
            ch.uw = _dot(ch.x, ch.rhs)
        for ch in chains:
            u_s[ch.step, :, ch.lanes] = ch.uw[:, :DN_DV]
            wq_s[ch.step, 0:c_, ch.lanes] = ch.uw[:, DN_DV:].astype(BF16)
        return carry

    lax.fori_loop(0, n_chunks // par, phase_a, 0)

    s_s[...] = jnp.zeros_like(s_s)
    oacc_s[...] = jnp.zeros_like(oacc_s)
    w2 = N_DIRS * DN_DV
    lane2 = lax.broadcasted_iota(jnp.int32, (c_, w2), 1)
    blockdiag = (lax.broadcasted_iota(jnp.int32, (w2, w2), 0) >> 7) == (
        lax.broadcasted_iota(jnp.int32, (w2, w2), 1) >> 7)

    def scan_step(s, carry):
        s_bf = s_s[...].astype(BF16)
        wqs = _dot(wq_s[s], s_bf)
        vn = u_s[s] - wqs[:c_]
        vn_bf = vn.astype(BF16)
        rhs = jnp.concatenate([jnp.where(lane2 < DN_DV, vn, 0.0).astype(BF16),
                               jnp.where(lane2 >= DN_DV, vn, 0.0).astype(BF16)], axis=0)
        o = wqs[c_:] + _dot(att_s[s], rhs)
        kv = _dot(kdt_s[s], vn_bf)
        s_s[...] = s_s[...] * gam_s[s] + jnp.where(blockdiag, kv, 0.0)
        rf = pl.multiple_of(s * c_, c_)
        rb = pl.multiple_of((n_chunks - 1 - s) * c_, c_)
        oacc_s[pl.ds(rf, c_), :] += o[:, :DN_DV]
        oacc_s[pl.ds(rb, c_), :] += o[:, DN_DV:]
        return carry

    lax.fori_loop(0, n_chunks, scan_step, 0)

    o = oacc_s[...]
    ms = jnp.mean(o * o, axis=-1, keepdims=True)
    o_ref[...] = o * lax.rsqrt(ms + RMS_EPS) * nw_ref[...] * z_ref[...]


def _delta(layer, main, st, gct, norm_w):
    b, seq, _ = main.shape
    c_ = DELTA_CHUNK
    n_chunks = seq // c_
    n_bd = N_DIRS * DN_HEADS
    n_lvl = len(_INV_LEVELS)
    w2 = N_DIRS * DN_DV
    kcol = DN_KEY // DN_DK
    head_spec = lambda off: pl.BlockSpec((None, seq, DN_DK), lambda i, h, l: (i, 0, off + h))
    grid_spec = pltpu.PrefetchScalarGridSpec(
        num_scalar_prefetch=1,
        grid=(b, DN_HEADS),
        in_specs=[
            head_spec(0), head_spec(kcol), head_spec(2 * kcol), head_spec(3 * kcol),
            pl.BlockSpec((None, 3, seq, LANES), lambda i, h, l: (i, 0, 0, 0)),
            pl.BlockSpec((None, n_chunks, n_bd, c_), lambda i, h, l: (i, 0, 0, 0)),
            pl.BlockSpec((None, 1, DN_DV), lambda i, h, l: (l[0], 0, 0)),
        ],
        out_specs=pl.BlockSpec((None, seq, DN_DV), lambda i, h, l: (i, 0, h)),
        scratch_shapes=[
            pltpu.VMEM((n_chunks, 2 * c_, w2), BF16),
            pltpu.VMEM((n_chunks, c_, w2), F32),
            pltpu.VMEM((n_chunks, c_, N_DIRS * c_), BF16),
            pltpu.VMEM((n_chunks, N_DIRS * DN_DK, c_), BF16),
            pltpu.VMEM((n_chunks, N_DIRS * DN_DK, w2), F32),
            pltpu.VMEM((N_DIRS * DN_DK, w2), F32),
            pltpu.VMEM((seq, DN_DV), F32),
            pltpu.VMEM((N_DIRS * n_lvl, c_, c_), BF16),
        ],
    )
    return pl.pallas_call(
        functools.partial(_delta_body, seq=seq), grid_spec=grid_spec,
        out_shape=jax.ShapeDtypeStruct((b, seq, DN_VAL), F32),
        compiler_params=_params(2), name="delta",
    )(layer, main, main, main, main, st, gct, norm_w)


def _dft_tables(seq):
    n = 2 * seq
    k = lax.broadcasted_iota(jnp.int32, (seq, seq), 0)
    t = lax.broadcasted_iota(jnp.int32, (seq, seq), 1)
    ang = ((k * t) & (n - 1)).astype(F32) * (2.0 * math.pi / n)
    return jnp.cos(ang).astype(BF16), jnp.sin(ang).astype(BF16)


def _filter_feats(seq):
    t = jnp.linspace(0.0, 1.0, seq, dtype=F32)[:, None]
    bands = (HY_EMB - 1) // 2
    ang = ((2.0 * math.pi / seq) * jnp.arange(seq, dtype=F32)[:, None]
           * jnp.linspace(1e-4, bands - 1, bands, dtype=F32)[None, :])
    feats = jnp.concatenate([t, jnp.cos(ang), -jnp.sin(ang)], axis=-1)
    return jnp.pad(feats, ((0, 0), (0, PAD_HID - HY_EMB)))


def _filter_deltas():
    max_decay = math.log(HY_DECAY_TARGET) / HY_FAST_DECAY_PCT
    min_decay = math.log(HY_DECAY_TARGET) / HY_SLOW_DECAY_PCT
    return jnp.abs(jnp.linspace(min_decay, max_decay, HY_WIDTH, dtype=F32))[None, :]


def _filt_body(layer_ref, feats_ref, w1_ref, b1_ref, w2_ref, b2_ref, w3_ref, b3_ref, fr_ref,
               wof_ref, wob_ref, dl_ref, c_ref, s_ref, kre_ref, kim_ref, hdn_ref, *, seq):
    del layer_ref
    n = 2 * seq

    @pl.when(pl.program_id(0) == 0)
    def _():
        fr = fr_ref[...]
        hp = functools.partial(jnp.dot, preferred_element_type=F32, precision=HIGHEST)
        h = jnp.sin(fr * (hp(feats_ref[...], w1_ref[...]) + b1_ref[...]))
        h = jnp.sin(fr * (hp(h, w2_ref[...]) + b2_ref[...]))
        hdn_ref[...] = jnp.sin(fr * (hp(h, w3_ref[...]) + b3_ref[...]))

    rowi = lax.broadcasted_iota(jnp.int32, (seq, HY_CT), 0)
    tt = rowi.astype(F32) * (1.0 / (seq - 1))
    window = jnp.exp(-tt * dl_ref[...])
    hdn = hdn_ref[...]
    fwd = jnp.dot(hdn, wof_ref[...], preferred_element_type=F32, precision=HIGHEST) * window
    bwd = jnp.dot(hdn, wob_ref[...], preferred_element_type=F32, precision=HIGHEST) * window
    bsh = jnp.where(rowi >= 1, pltpu.roll(bwd, 1, axis=0), 0.0)
    rhs = jnp.concatenate([fwd, bsh], axis=1).astype(BF16)
    gc = _dot(c_ref[...], rhs)
    gs = _dot(s_ref[...], rhs)
    alt = (1 - 2 * (rowi & 1)).astype(F32)
    nyq = jnp.sum(alt * (fwd + bsh), axis=0, keepdims=True)
    kre = gc[:, :HY_CT] + gc[:, HY_CT:]
    kim = gs[:, HY_CT:] - gs[:, :HY_CT]
    first = rowi == 0
    kre_ref[...] = kre * jnp.where(first, 1.0 / n, 2.0 / n)
    kim_ref[...] = jnp.where(first, nyq * (1.0 / n), kim * (2.0 / n))


def _filter_spectrum(layer, feats, w1, b1, w2, b2, w3, b3, freq, wout, deltas, ctab, stab):
    seq = feats.shape[0]
    n_oc = HY_ORDER * HY_WIDTH
    tiles = n_oc // HY_CT
    per_w = HY_WIDTH // HY_CT
    sq = lambda: pl.BlockSpec((None, PAD_HID, PAD_HID), lambda j, l: (l[0], 0, 0))
    vec = lambda: pl.BlockSpec((None, 1, PAD_HID), lambda j, l: (l[0], 0, 0))
    grid_spec = pltpu.PrefetchScalarGridSpec(
        num_scalar_prefetch=1,
        grid=(tiles,),
        in_specs=[
            pl.BlockSpec((seq, PAD_HID), lambda j, l: (0, 0)),
            sq(), vec(), sq(), vec(), sq(), vec(), vec(),
            pl.BlockSpec((None, PAD_HID, HY_CT), lambda j, l: (l[0], 0, j)),
            pl.BlockSpec((None, PAD_HID, HY_CT), lambda j, l: (l[0], 0, tiles + j)),
            pl.BlockSpec((1, HY_CT), lambda j, l: (0, j % per_w)),
            _resident((seq, seq), lambda j, l: (0, 0)),
            _resident((seq, seq), lambda j, l: (0, 0)),
        ],
        out_specs=[pl.BlockSpec((None, seq, HY_CT), lambda j, l: (j % per_w, 0, j // per_w)),
                   pl.BlockSpec((None, seq, HY_CT), lambda j, l: (j % per_w, 0, j // per_w))],
        scratch_shapes=[pltpu.VMEM((seq, PAD_HID), F32)],
    )
    return pl.pallas_call(
        functools.partial(_filt_body, seq=seq), grid_spec=grid_spec,
        out_shape=[jax.ShapeDtypeStruct((per_w, seq, HY_ORDER * HY_CT), F32)] * 2,
        compiler_params=_params(1), name="hyena_filter",
    )(layer, feats, w1, b1, w2, b2, w3, b3, freq, wout, wout, deltas, ctab, stab)


def _hyena_body(layer_ref, v_ref, x1_ref, x2_ref, kre_ref, kim_ref, skip_ref, c_ref, s_ref, o_ref,
                z_s, xc_s, xs_s, ya_s, yb_s, *, seq):
    del layer_ref
    rc = HY_ROWS
    n_rc = seq // rc
    alt = (1 - 2 * (lax.broadcasted_iota(jnp.int32, (rc, HY_CT), 0) & 1)).astype(F32)

    def rows(c):
        return pl.ds(pl.multiple_of(c * rc, rc), rc)

    for o, gate_ref in enumerate((x1_ref, x2_ref)):
        z_ref = v_ref if o == 0 else z_s
        last = o == HY_ORDER - 1
        cols = slice(o * HY_CT, (o + 1) * HY_CT)

        def to_bf16(c, acc):
            z = z_ref[rows(c), :]
            ya_s[rows(c), :] = z.astype(BF16)
            yb_s[rows(c), :] = z.astype(BF16)
            return acc + jnp.sum(alt * z, axis=0, keepdims=True)

        xnyq = lax.fori_loop(0, n_rc, to_bf16, jnp.zeros((1, HY_CT), F32))
        xc_s[...] = _dot(c_ref[...], ya_s[...])
        xs_s[...] = _dot(s_ref[...], yb_s[...])

        def spectrum(c, carry):
            xc, xs = xc_s[rows(c), :], xs_s[rows(c), :]
            kre, kim = kre_ref[rows(c), cols], kim_ref[rows(c), cols]
            ya_s[rows(c), :] = (xc * kre + xs * kim).astype(BF16)
            yb_s[rows(c), :] = (xs * kre - xc * kim).astype(BF16)
            return carry

        lax.fori_loop(0, n_rc, spectrum, 0)
        xc_s[...] = _dot(c_ref[...], ya_s[...])
        xs_s[...] = _dot(s_ref[...], yb_s[...])
        ynyq = xnyq * kim_ref[0:1, cols]
        skip = skip_ref[o:o + 1, :]

        def gate(c, carry):
            z = z_ref[rows(c), :]
            y = xc_s[rows(c), :] + xs_s[rows(c), :] + alt * ynyq
            z_new = gate_ref[rows(c), :] * (y + skip * z)
            (o_ref if last else z_s)[rows(c), :] = z_new
            return carry

        lax.fori_loop(0, n_rc, gate, 0)


def _hyena(layer, main, kre, kim, skip, ctab, stab):
    b, seq, _ = main.shape
    tiles = HY_WIDTH // HY_CT
    col0 = (2 * DN_KEY + 2 * DN_VAL) // HY_CT
    hy_spec = lambda part: pl.BlockSpec(
        (None, seq, HY_CT), lambda j, i, l: (i, 0, col0 + part * tiles + j))
    k_spec = lambda: pl.BlockSpec((None, seq, HY_ORDER * HY_CT), lambda j, i, l: (j, 0, 0),
                                  pipeline_mode=pl.Buffered(1))
    grid_spec = pltpu.PrefetchScalarGridSpec(
        num_scalar_prefetch=1,
        grid=(tiles, b),
        in_specs=[
            hy_spec(0), hy_spec(1), hy_spec(2), k_spec(), k_spec(),
            pl.BlockSpec((None, None, HY_ORDER, HY_CT), lambda j, i, l: (l[0], j, 0, 0)),
            _resident((seq, seq), lambda j, i, l: (0, 0)),
            _resident((seq, seq), lambda j, i, l: (0, 0)),
        ],
        out_specs=pl.BlockSpec((None, seq, HY_CT), lambda j, i, l: (i, 0, j)),
        scratch_shapes=[pltpu.VMEM((seq, HY_CT), F32),
                        pltpu.VMEM((seq, HY_CT), F32), pltpu.VMEM((seq, HY_CT), F32),
                        pltpu.VMEM((seq, HY_CT), BF16), pltpu.VMEM((seq, HY_CT), BF16)],
    )
    return pl.pallas_call(
        functools.partial(_hyena_body, seq=seq), grid_spec=grid_spec,
        out_shape=jax.ShapeDtypeStruct((b, seq, HY_WIDTH), F32),
        compiler_params=_params(2), name="hyena",
    )(layer, main, main, main, kre, kim, skip, ctab, stab)


def _mixout_body(layer_ref, x_ref, odn_ref, zz_ref, hn_ref, wo_ref, o_ref):
    del layer_ref
    zz = zz_ref[...]
    ms = jnp.mean(zz * zz, axis=-1, keepdims=True)
    zn = zz * lax.rsqrt(ms + RMS_EPS) * hn_ref[...]
    y = _dot(odn_ref[...].astype(BF16), wo_ref[0:DN_VAL, :])
    y = y + _dot(zn.astype(BF16), wo_ref[DN_VAL:D_MODEL, :])
    o_ref[...] = x_ref[...] + y


def _mixout(layer, x2d, odn2d, zz2d, hy_norm, w_out, *, tm):
    m = x2d.shape[0]
    grid_spec = pltpu.PrefetchScalarGridSpec(
        num_scalar_prefetch=1,
        grid=(m // tm,),
        in_specs=[
            pl.BlockSpec((tm, D_MODEL), lambda i, l: (i, 0)),
            pl.BlockSpec((tm, DN_VAL), lambda i, l: (i, 0)),
            pl.BlockSpec((tm, HY_WIDTH), lambda i, l: (i, 0)),
            pl.BlockSpec((None, 1, HY_WIDTH), lambda i, l: (l[0], 0, 0)),
            pl.BlockSpec((None, D_MODEL, D_MODEL), lambda i, l: (l[0], 0, 0)),
        ],
        out_specs=pl.BlockSpec((tm, D_MODEL), lambda i, l: (i, 0)),
    )
    return pl.pallas_call(
        _mixout_body, grid_spec=grid_spec,
        out_shape=jax.ShapeDtypeStruct((m, D_MODEL), F32),
        compiler_params=_params(1), name="mixout",
    )(layer, x2d, odn2d, zz2d, hy_norm, w_out)


def _final_norm_body(x_ref, w_ref, o_ref):
    x = x_ref[...]
    ms = jnp.mean(x * x, axis=-1, keepdims=True)
    o_ref[...] = x * lax.rsqrt(ms + RMS_EPS) * w_ref[...]


def _final_norm(x2d, w, *, tm):
    m = x2d.shape[0]
    return pl.pallas_call(
        _final_norm_body, grid=(m // tm,),
        in_specs=[pl.BlockSpec((tm, D_MODEL), lambda i: (i, 0)),
                  pl.BlockSpec((1, D_MODEL), lambda i: (0, 0))],
        out_specs=pl.BlockSpec((tm, D_MODEL), lambda i: (i, 0)),
        out_shape=jax.ShapeDtypeStruct((m, D_MODEL), F32),
        compiler_params=_params(1), name="final_norm",
    )(x2d, w)


def _pad_to(a, axis, size):
    pads = [(0, 0)] * a.ndim
    pads[axis] = (0, size - a.shape[axis])
    return jnp.pad(a, pads)


def _row_tile(m):
    return 1024 if m % 1024 == 0 else m


def kernel(x, ffn1_norm, ffn1_w_gate, ffn1_w_up, ffn1_w_down, mix_norm, w_in, dn_conv, dn_a_log, dn_dt_bias, dn_norm, hy_conv, hy_conv_bias, hy_f_w1, hy_f_b1, hy_f_w2, hy_f_b2, hy_f_w3, hy_f_b3, hy_f_freq, hy_f_wout, hy_skip, hy_norm, w_out, ffn2_norm, ffn2_w_gate, ffn2_w_up, ffn2_w_down, final_norm):
    b, seq, _ = x.shape
    m = b * seq
    tm = _row_tile(m)
    n_bd = N_DIRS * DN_HEADS
    bf = lambda a: a.astype(BF16)
    row3 = lambda a: a[:, None, :]

    ba0 = 2 * DN_KEY + 2 * DN_VAL
    w_main = bf(jnp.concatenate([w_in[:, :, :ba0], w_in[:, :, ba0 + 2 * n_bd:]], axis=-1))
    w_ba = bf(_pad_to(w_in[:, :, ba0:ba0 + 2 * n_bd], 2, LANES))
    lane_params = lambda p: _pad_to(
        jnp.pad(p.reshape(DEPTH, 1, n_bd), ((0, 0), (0, 0), (n_bd, 0))), 2, LANES)
    a_log = lane_params(dn_a_log)
    dt_bias = lane_params(dn_dt_bias)
    f_w1 = _pad_to(_pad_to(hy_f_w1, 1, PAD_HID), 2, PAD_HID)
    f_w2 = _pad_to(_pad_to(hy_f_w2, 1, PAD_HID), 2, PAD_HID)
    f_w3 = _pad_to(_pad_to(hy_f_w3, 1, PAD_HID), 2, PAD_HID)
    f_b1, f_b2, f_b3, f_fr = (row3(_pad_to(p, 1, PAD_HID))
                              for p in (hy_f_b1, hy_f_b2, hy_f_b3, hy_f_freq))
    f_wout = _pad_to(hy_f_wout, 1, PAD_HID)
    tiles = HY_WIDTH // HY_CT
    skip = hy_skip.reshape(DEPTH, HY_ORDER, tiles, HY_CT).transpose(0, 2, 1, 3)
    weights = dict(
        ffn1=(row3(ffn1_norm), bf(ffn1_w_gate), bf(ffn1_w_up), bf(ffn1_w_down)),
        ffn2=(row3(ffn2_norm), bf(ffn2_w_gate), bf(ffn2_w_up), bf(ffn2_w_down)),
        w_out=bf(w_out))

    ctab, stab = _dft_tables(seq)
    feats = _filter_feats(seq)
    deltas = _filter_deltas()

    def layer_fn(i, carry):
        xc, ctab, stab = carry
        layer = jnp.full((1,), i, jnp.int32)
        x2 = _ffn(layer, xc.reshape(m, D_MODEL), *weights["ffn1"], tm=tm)
        x3 = x2.reshape(b, seq, D_MODEL)
        main, st, gct = _mixin(layer, x3, row3(mix_norm), w_main, w_ba, dn_conv, hy_conv,
                               row3(hy_conv_bias), a_log, dt_bias)
        kre, kim = _filter_spectrum(layer, feats, f_w1, f_b1, f_w2, f_b2, f_w3, f_b3, f_fr,
                                    f_wout, deltas, ctab, stab)
        o_dn = _delta(layer, main, st, gct, row3(dn_norm))
        zz = _hyena(layer, main, kre, kim, skip, ctab, stab)
        x4 = _mixout(layer, x2, o_dn.reshape(m, DN_VAL), zz.reshape(m, HY_WIDTH),
                     row3(hy_norm), weights["w_out"], tm=tm)
        x5 = _ffn(layer, x4, *weights["ffn2"], tm=tm)
        return x5.reshape(b, seq, D_MODEL), ctab, stab

    xf, _, _ = lax.fori_loop(0, DEPTH, layer_fn, (x, ctab, stab))
    return _final_norm(xf.reshape(m, D_MODEL), final_norm[None, :], tm=tm).reshape(b, seq, D_MODEL)
```

```python
import functools
import itertools
import math
import types

import jax
import jax.numpy as jnp
from jax import lax
from jax.experimental import pallas as pl
from jax.experimental.pallas import tpu as pltpu

F32 = jnp.float32
BF16 = jnp.bfloat16
HIGHEST = lax.Precision.HIGHEST

D_MODEL = 1024
DEPTH = 4
N_DIRS = 2
DN_HEADS = 4
DN_DK = 128
DN_DV = 128
DN_KEY = DN_HEADS * DN_DK
DN_VAL = DN_HEADS * DN_DV
DN_CONV = 5
HY_WIDTH = D_MODEL - DN_VAL
HY_ORDER = 2
HY_SHORT = 3
HY_EMB = 33
HY_HID = 64
HY_DIRS = 2
HY_FAST_DECAY_PCT = 0.3
HY_SLOW_DECAY_PCT = 1.5
HY_DECAY_TARGET = 1e-2
D_FF = 2816
RMS_EPS = 1e-6

LANES = 128
SUBLANES = 8
MXU_DIM = 256
VMEM_LIMIT_BYTES = 58 * 1024 * 1024

N_MAIN = 2 * DN_KEY + 2 * DN_VAL + 3 * HY_WIDTH
COL_TILE = 512
N_COL_TILES = N_MAIN // COL_TILE
HY_COL0 = (2 * DN_KEY + 2 * DN_VAL) // COL_TILE
FF_TILE = MXU_DIM
N_FF_TILES = D_FF // FF_TILE
DELTA_CHUNK = MXU_DIM
HEADS_PER_STEP = 2
PAR_CHUNKS = 4
HY_CT = MXU_DIM
HY_ROWS = 256
PAD_HID = LANES


def _params(n_grid):
    return pltpu.CompilerParams(
        dimension_semantics=("arbitrary",) * n_grid,
        vmem_limit_bytes=VMEM_LIMIT_BYTES)


def _resident(block, imap):
    return pl.BlockSpec(block, imap, pipeline_mode=pl.Buffered(1))


def _dot(a, b):
    return jnp.dot(a, b, preferred_element_type=F32)


def _silu(x):
    return x * jax.nn.sigmoid(x)


def _ffn_body(layer_ref, x_ref, nw_ref, wg_ref, wu_ref, wd_ref, o_ref, h_ref, a_ref):
    del layer_ref
    x = x_ref[...]
    ms = jnp.mean(x * x, axis=-1, keepdims=True)
    h_ref[...] = (x * lax.rsqrt(ms + RMS_EPS) * nw_ref[...]).astype(BF16)
    for f in range(N_FF_TILES):
        cols = slice(f * FF_TILE, (f + 1) * FF_TILE)
        g = _dot(h_ref[...], wg_ref[:, cols])
        u = _dot(h_ref[...], wu_ref[:, cols])
        a_ref[:, cols] = (_silu(g) * u).astype(BF16)
    o_ref[...] = x_ref[...] + 0.5 * _dot(a_ref[...], wd_ref[...])


def _ffn(layer, x2d, norm_w, wg, wu, wd, *, tm):
    m = x2d.shape[0]
    grid_spec = pltpu.PrefetchScalarGridSpec(
        num_scalar_prefetch=1,
        grid=(m // tm,),
        in_specs=[
            pl.BlockSpec((tm, D_MODEL), lambda i, l: (i, 0)),
            pl.BlockSpec((None, 1, D_MODEL), lambda i, l: (l[0], 0, 0)),
            _resident((None, D_MODEL, D_FF), lambda i, l: (l[0], 0, 0)),
            _resident((None, D_MODEL, D_FF), lambda i, l: (l[0], 0, 0)),
            _resident((None, D_FF, D_MODEL), lambda i, l: (l[0], 0, 0)),
        ],
        out_specs=pl.BlockSpec((tm, D_MODEL), lambda i, l: (i, 0)),
        scratch_shapes=[pltpu.VMEM((tm, D_MODEL), BF16), pltpu.VMEM((tm, D_FF), BF16)],
    )
    return pl.pallas_call(
        _ffn_body, grid_spec=grid_spec,
        out_shape=jax.ShapeDtypeStruct((m, D_MODEL), F32),
        compiler_params=_params(1), name="ffn",
    )(layer, x2d, norm_w, wg, wu, wd)


def _softplus(x):
    return jnp.maximum(x, 0.0) + jnp.log1p(jnp.exp(-jnp.abs(x)))


def _mixin_body(layer_ref, x_ref, nw_ref, wm_ref, wba_ref, dnc_ref, hyc_ref, hyb_ref,
                alog_ref, dtb_ref, main_ref, st_ref, gct_ref, h_ref, cbuf_ref, *, seq, chunk):
    del layer_ref
    j = pl.program_id(1)
    n_chunks = seq // chunk
    pad = SUBLANES

    @pl.when(j == 0)
    def _():
        x = x_ref[...]
        ms = jnp.mean(x * x, axis=-1, keepdims=True)
        h_ref[...] = (x * lax.rsqrt(ms + RMS_EPS) * nw_ref[...]).astype(BF16)
        zeros = jnp.zeros((pad, COL_TILE), F32)
        cbuf_ref[0:pad, :] = zeros
        cbuf_ref[pad + seq:pad + seq + pad, :] = zeros

        ba = _dot(h_ref[...], wba_ref[...])
        lane = lax.broadcasted_iota(jnp.int32, (seq, LANES), 1)
        rowc = lax.broadcasted_iota(jnp.int32, (seq, LANES), 0) & (chunk - 1)
        n_bd = N_DIRS * DN_HEADS
        beta = jax.nn.sigmoid(ba)
        g = -jnp.exp(alog_ref[...]) * _softplus(ba + dtb_ref[...])
        g = jnp.where((lane >= n_bd) & (lane < 2 * n_bd), g, 0.0)
        pre = g
        suf = g
        s = 1
        while s < chunk:
            pre = pre + jnp.where(rowc >= s, pltpu.roll(pre, s, axis=0), 0.0)
            suf = suf + jnp.where(rowc < chunk - s, pltpu.roll(suf, seq - s, axis=0), 0.0)
            s *= 2
        is_fwd = lane < n_bd + DN_HEADS
        gc = jnp.where(is_fwd, pre, suf)
        pre3 = pre.reshape(n_chunks, chunk, LANES)
        suf3 = suf.reshape(n_chunks, chunk, LANES)
        tot_f = jnp.broadcast_to(pre3[:, chunk - 1:chunk, :], pre3.shape).reshape(seq, LANES)
        tot_b = jnp.broadcast_to(suf3[:, 0:1, :], suf3.shape).reshape(seq, LANES)
        gl = jnp.where(is_fwd, tot_f, tot_b)
        st0 = jnp.where(lane < n_bd, beta, gc)
        st_ref[0] = st0
        st_ref[1] = gl - gc
        st_ref[2] = gl
        gct = st0.T
        for c in range(n_chunks):
            gct_ref[c] = gct[n_bd:2 * n_bd, c * chunk:(c + 1) * chunk]

    cbuf_ref[pad:pad + seq, :] = _dot(h_ref[...], wm_ref[...])

    def conv_rows(r0, w_ref, k):
        win = cbuf_ref[pl.ds(r0, chunk + 2 * pad), :]
        acc = None
        for t in range(k):
            d = t - k // 2
            rolled = win if d == 0 else pltpu.roll(win, (-d) % (chunk + 2 * pad), axis=0)
            term = w_ref[t:t + 1, :] * rolled[pad:pad + chunk]
            acc = term if acc is None else acc + term
        return acc

    def l2norm_heads(t):
        outs = []
        for hh in range(COL_TILE // DN_DK):
            th = t[:, hh * DN_DK:(hh + 1) * DN_DK]
            outs.append(th * lax.rsqrt(jnp.sum(th * th, axis=-1, keepdims=True) + 1e-6))
        return jnp.concatenate(outs, axis=1)

    def rows(c):
        return pl.multiple_of(c * chunk, chunk)

    @pl.when(j == 0)
    def _():
        def body(c, carry):
            r0 = rows(c)
            t = l2norm_heads(_silu(conv_rows(r0, dnc_ref, DN_CONV)))
            main_ref[pl.ds(r0, chunk), :] = t * (DN_DK ** -0.5)
            return carry
        lax.fori_loop(0, n_chunks, body, 0)

    @pl.when(j == 1)
    def _():
        def body(c, carry):
            r0 = rows(c)
            main_ref[pl.ds(r0, chunk), :] = l2norm_heads(_silu(conv_rows(r0, dnc_ref, DN_CONV)))
            return carry
        lax.fori_loop(0, n_chunks, body, 0)

    @pl.when(j == 2)
    def _():
        def body(c, carry):
            r0 = rows(c)
            main_ref[pl.ds(r0, chunk), :] = _silu(conv_rows(r0, dnc_ref, DN_CONV))
            return carry
        lax.fori_loop(0, n_chunks, body, 0)

    @pl.when(j == 3)
    def _():
        def body(c, carry):
            r0 = rows(c)
            main_ref[pl.ds(r0, chunk), :] = _silu(cbuf_ref[pl.ds(r0 + pad, chunk), :])
            return carry
        lax.fori_loop(0, n_chunks, body, 0)

    @pl.when(j >= HY_COL0)
    def _():
        def body(c, carry):
            r0 = rows(c)
            main_ref[pl.ds(r0, chunk), :] = conv_rows(r0, hyc_ref, HY_SHORT) + hyb_ref[...]
            return carry
        lax.fori_loop(0, n_chunks, body, 0)


def _mixin(layer, x, norm_w, w_main, w_ba, dn_conv, hy_conv, hy_bias, a_log, dt_bias):
    b, seq, _ = x.shape
    chunk = DELTA_CHUNK
    n_chunks = seq // chunk
    n_bd = N_DIRS * DN_HEADS
    dn_tiles = (2 * DN_KEY + DN_VAL) // COL_TILE
    hy_tiles = 3 * HY_WIDTH // COL_TILE
    grid_spec = pltpu.PrefetchScalarGridSpec(
        num_scalar_prefetch=1,
        grid=(b, N_COL_TILES),
        in_specs=[
            pl.BlockSpec((None, seq, D_MODEL), lambda i, j, l: (i, 0, 0)),
            pl.BlockSpec((None, 1, D_MODEL), lambda i, j, l: (l[0], 0, 0)),
            pl.BlockSpec((None, D_MODEL, COL_TILE), lambda i, j, l: (l[0], 0, j)),
            pl.BlockSpec((None, D_MODEL, LANES), lambda i, j, l: (l[0], 0, 0)),
            pl.BlockSpec((None, DN_CONV, COL_TILE),
                         lambda i, j, l: (l[0], 0, jnp.minimum(j, dn_tiles - 1))),
            pl.BlockSpec((None, HY_SHORT, COL_TILE),
                         lambda i, j, l: (l[0], 0, jnp.clip(j - HY_COL0, 0, hy_tiles - 1))),
            pl.BlockSpec((None, 1, COL_TILE),
                         lambda i, j, l: (l[0], 0, jnp.clip(j - HY_COL0, 0, hy_tiles - 1))),
            pl.BlockSpec((None, 1, LANES), lambda i, j, l: (l[0], 0, 0)),
            pl.BlockSpec((None, 1, LANES), lambda i, j, l: (l[0], 0, 0)),
        ],
        out_specs=[
            pl.BlockSpec((None, seq, COL_TILE), lambda i, j, l: (i, 0, j)),
            pl.BlockSpec((None, 3, seq, LANES), lambda i, j, l: (i, 0, 0, 0)),
            pl.BlockSpec((None, n_chunks, n_bd, chunk), lambda i, j, l: (i, 0, 0, 0)),
        ],
        scratch_shapes=[pltpu.VMEM((seq, D_MODEL), BF16),
                        pltpu.VMEM((seq + 2 * SUBLANES, COL_TILE), F32)],
    )
    return pl.pallas_call(
        functools.partial(_mixin_body, seq=seq, chunk=chunk), grid_spec=grid_spec,
        out_shape=[jax.ShapeDtypeStruct((b, seq, N_MAIN), F32),
                   jax.ShapeDtypeStruct((b, 3, seq, LANES), F32),
                   jax.ShapeDtypeStruct((b, n_chunks, n_bd, chunk), F32)],
        compiler_params=_params(2), name="mixin",
    )(layer, x, norm_w, w_main, w_ba, dn_conv, hy_conv, hy_bias, a_log, dt_bias)


_INV_LEVELS = (8, 16, 32, 64, 128)


def _delta_body(layer_ref, q_ref, k_ref, v_ref, z_ref, st_ref, gct_ref, nw_ref, o_ref,
                wq_s, u_s, att_s, kdt_s, gam_s, s_s, oacc_s, *, seq):
    del layer_ref
    hp = pl.program_id(1)
    heads = range(HEADS_PER_STEP)
    c_ = DELTA_CHUNK
    n_chunks = seq // c_
    n_bd = N_DIRS * DN_HEADS
    row = lax.broadcasted_iota(jnp.int32, (c_, c_), 0)
    col = lax.broadcasted_iota(jnp.int32, (c_, c_), 1)
    lane = lax.broadcasted_iota(jnp.int32, (c_, LANES), 1)

    half_row = lax.broadcasted_iota(jnp.int32, (c_ // 2, c_), 0)
    half_col = lax.broadcasted_iota(jnp.int32, (c_ // 2, c_), 1)

    def pick(x, idx):
        return jnp.sum(jnp.where(lane == idx, x, 0.0), axis=1, keepdims=True)

    par = math.gcd(max(PAR_CHUNKS // HEADS_PER_STEP, 1), n_chunks)

    def phase_a(it, carry):
        chains = []
        for hh, j in itertools.product(heads, range(par)):
            hd = hp * HEADS_PER_STEP + hh
            head_lanes = slice(hh * DN_DK, (hh + 1) * DN_DK)
            g = it * par + j
            r0 = pl.multiple_of(g * c_, c_)
            q = q_ref[pl.ds(r0, c_), head_lanes]
            k = k_ref[pl.ds(r0, c_), head_lanes]
            v = v_ref[pl.ds(r0, c_), head_lanes]
            st0 = st_ref[0, pl.ds(r0, c_), :]
            st1 = st_ref[1, pl.ds(r0, c_), :]
            st2 = st_ref[2, pl.ds(r0, c_), :]
            k_bf = k.astype(BF16)
            q_bf = q.astype(BF16)
            for d in range(N_DIRS):
                ch = types.SimpleNamespace(d=d, step=g if d == 0 else n_chunks - 1 - g, hh=hh,
                                           lanes=slice(d * DN_DV, (d + 1) * DN_DV), k_bf=k_bf)
                bi = d * DN_HEADS + hd
                beta = pick(st0, bi)
                gc = pick(st0, n_bd + bi)
                egc = jnp.exp(gc)
                ekd = jnp.exp(pick(st1, n_bd + bi))
                egl = jnp.exp(pick(st2, n_bd + bi))
                gcr = gct_ref[g, pl.ds(bi, 1), :]
                incl = (row >= col) if d == 0 else (row <= col)
                ch.strict = (row > col) if d == 0 else (row < col)
                ch.dec = jnp.where(incl, jnp.exp(jnp.where(incl, gc - gcr, 0.0)), 0.0)
                kb = k * beta
                ch.lhs = jnp.concatenate([kb.astype(BF16), q_bf], axis=0)
                ch.rhs = jnp.concatenate([(v * beta).astype(BF16), (kb * egc).astype(BF16)], axis=1)
                wq_s[hh, ch.step, c_:2 * c_, ch.lanes] = (q * egc).astype(BF16)
                kdt_s[hh, ch.step, d * DN_DK:(d + 1) * DN_DK, :] = (k * ekd).T.astype(BF16)
                gam_s[hh, ch.step, d * DN_DK:(d + 1) * DN_DK, :] = jnp.broadcast_to(
                    egl[0:DN_DK, :], (DN_DK, N_DIRS * DN_DV))
                chains.append(ch)

        for ch in chains:
            ch.s1 = lax.dot_general(ch.lhs, ch.k_bf, (((1,), (1,)), ((), ())),
                                    preferred_element_type=F32)
        for ch in chains:
            a = jnp.where(ch.strict, ch.s1[:c_] * ch.dec, 0.0)
            att_s[ch.hh, ch.step, :, ch.d * c_:(ch.d + 1) * c_] = (ch.s1[c_:] * ch.dec).astype(BF16)
            ch.a = a
            ch.b0 = jnp.where((row >> 3) == (col >> 3), -a, 0.0)
            ch.b0_bf = ch.b0.astype(BF16)
        for ch in chains:
            ch.r1 = _dot(ch.b0_bf, ch.b0_bf)
        for ch in chains:
            r1_bf = ch.r1.astype(BF16)
            ch.m = _dot(jnp.concatenate([ch.b0_bf, r1_bf], axis=0), r1_bf)
        for ch in chains:
            ch.q2 = ch.b0 + ch.r1 + ch.m[:c_]
            ch.r2 = ch.m[c_:]
            ch.q2r2 = _dot(ch.q2.astype(BF16), ch.r2.astype(BF16))
        for ch in chains:
            ch.xf = jnp.where(row == col, 1.0, ch.q2 + ch.r2 + ch.q2r2)
            ch.x = ch.xf.astype(BF16)
        for s in _INV_LEVELS:
            sh = s.bit_length() - 1
            n_pairs = c_ // (2 * s)

            def take(m, odd, s=s, n_pairs=n_pairs):
                return jnp.concatenate(
                    [m[(2 * i + odd) * s:(2 * i + odd + 1) * s] for i in range(n_pairs)], axis=0)

            def interleave(even, odd, s=s, n_pairs=n_pairs):
                pieces = []
                for i in range(n_pairs):
                    pieces += [even[i * s:(i + 1) * s], odd[i * s:(i + 1) * s]]
                return jnp.concatenate(pieces, axis=0)

            for ch in chains:
                act = 1 - ch.d
                partner_cols = (half_col >> sh) == (((half_row >> sh) << 1) + ch.d)
                e = jnp.where(partner_cols, take(ch.a, act), 0.0)
                ch.m1 = _dot(e.astype(BF16), ch.x)
            for ch in chains:
                act = 1 - ch.d
                zeros = jnp.zeros_like(ch.m1)
                m1_rows = interleave(zeros, -ch.m1) if act else interleave(-ch.m1, zeros)
                ch.m2 = _dot(take(ch.xf, act).astype(BF16), m1_rows.astype(BF16))
            for ch in chains:
                act = 1 - ch.d
                new = take(ch.xf, act) + ch.m2
                keep = take(ch.xf, ch.d)
                ch.xf = interleave(keep, new) if act else interleave(new, keep)
                ch.x = ch.xf.astype(BF16)
        for ch in chains:
            ch.uw = _dot(ch.x, ch.rhs)
        for ch in chains:
            u_s[ch.hh, ch.step, :, ch.lanes] = ch.uw[:, :DN_DV]
            wq_s[ch.hh, ch.step, 0:c_, ch.lanes] = ch.uw[:, DN_DV:].astype(BF16)
        return carry

    lax.fori_loop(0, n_chunks // par, phase_a, 0)

    s_s[...] = jnp.zeros_like(s_s)
    oacc_s[...] = jnp.zeros_like(oacc_s)
    w2 = N_DIRS * DN_DV
    lane2 = lax.broadcasted_iota(jnp.int32, (c_, w2), 1)
    blockdiag = (lax.broadcasted_iota(jnp.int32, (w2, w2), 0) >> 7) == (
        lax.broadcasted_iota(jnp.int32, (w2, w2), 1) >> 7)

    def scan_step(s, carry):
        rf = pl.multiple_of(s * c_, c_)
        rb = pl.multiple_of((n_chunks - 1 - s) * c_, c_)
        wqs = [_dot(wq_s[hh, s], s_s[hh].astype(BF16)) for hh in heads]
        vn = [u_s[hh, s] - wqs[hh][:c_] for hh in heads]
        kv = [_dot(kdt_s[hh, s], vn[hh].astype(BF16)) for hh in heads]
        for hh in heads:
            s_s[hh] = s_s[hh] * gam_s[hh, s] + jnp.where(blockdiag, kv[hh], 0.0)
        for hh in heads:
            rhs = jnp.concatenate([jnp.where(lane2 < DN_DV, vn[hh], 0.0).astype(BF16),
                                   jnp.where(lane2 >= DN_DV, vn[hh], 0.0).astype(BF16)], axis=0)
            o = wqs[hh][c_:] + _dot(att_s[hh, s], rhs)
            oacc_s[hh, pl.ds(rf, c_), :] += o[:, :DN_DV]
            oacc_s[hh, pl.ds(rb, c_), :] += o[:, DN_DV:]
        return carry

    lax.fori_loop(0, n_chunks, scan_step, 0)

    for hh in heads:
        head_lanes = slice(hh * DN_DV, (hh + 1) * DN_DV)
        o = oacc_s[hh]
        ms = jnp.mean(o * o, axis=-1, keepdims=True)
        o_ref[:, head_lanes] = o * lax.rsqrt(ms + RMS_EPS) * nw_ref[...] * z_ref[:, head_lanes]


def _delta(layer, main, st, gct, norm_w):
    b, seq, _ = main.shape
    c_ = DELTA_CHUNK
    n_chunks = seq // c_
    n_bd = N_DIRS * DN_HEADS
    w2 = N_DIRS * DN_DV
    hps = HEADS_PER_STEP
    groups = DN_HEADS // hps
    head_spec = lambda part: pl.BlockSpec(
        (None, seq, hps * DN_DK), lambda i, h, l: (i, 0, part * groups + h))
    grid_spec = pltpu.PrefetchScalarGridSpec(
        num_scalar_prefetch=1,
        grid=(b, groups),
        in_specs=[
            head_spec(0), head_spec(1), head_spec(2), head_spec(3),
            pl.BlockSpec((None, 3, seq, LANES), lambda i, h, l: (i, 0, 0, 0)),
            pl.BlockSpec((None, n_chunks, n_bd, c_), lambda i, h, l: (i, 0, 0, 0)),
            pl.BlockSpec((None, 1, DN_DV), lambda i, h, l: (l[0], 0, 0)),
        ],
        out_specs=pl.BlockSpec((None, seq, hps * DN_DV), lambda i, h, l: (i, 0, h)),
        scratch_shapes=[
            pltpu.VMEM((hps, n_chunks, 2 * c_, w2), BF16),
            pltpu.VMEM((hps, n_chunks, c_, w2), F32),
            pltpu.VMEM((hps, n_chunks, c_, N_DIRS * c_), BF16),
            pltpu.VMEM((hps, n_chunks, N_DIRS * DN_DK, c_), BF16),
            pltpu.VMEM((hps, n_chunks, N_DIRS * DN_DK, w2), F32),
            pltpu.VMEM((hps, N_DIRS * DN_DK, w2), F32),
            pltpu.VMEM((hps, seq, DN_DV), F32),
        ],
    )
    return pl.pallas_call(
        functools.partial(_delta_body, seq=seq), grid_spec=grid_spec,
        out_shape=jax.ShapeDtypeStruct((b, seq, DN_VAL), F32),
        compiler_params=_params(2), name="delta",
    )(layer, main, main, main, main, st, gct, norm_w)


def _dft_tables(seq):
    n = 2 * seq
    k = lax.broadcasted_iota(jnp.int32, (seq, seq), 0)
    t = lax.broadcasted_iota(jnp.int32, (seq, seq), 1)
    ang = ((k * t) & (n - 1)).astype(F32) * (2.0 * math.pi / n)
    return jnp.cos(ang).astype(BF16), jnp.sin(ang).astype(BF16)


def _filter_feats(seq):
    t = jnp.linspace(0.0, 1.0, seq, dtype=F32)[:, None]
    bands = (HY_EMB - 1) // 2
    ang = ((2.0 * math.pi / seq) * jnp.arange(seq, dtype=F32)[:, None]
           * jnp.linspace(1e-4, bands - 1, bands, dtype=F32)[None, :])
    feats = jnp.concatenate([t, jnp.cos(ang), -jnp.sin(ang)], axis=-1)
    return jnp.pad(feats, ((0, 0), (0, PAD_HID - HY_EMB)))


def _filter_deltas():
    max_decay = math.log(HY_DECAY_TARGET) / HY_FAST_DECAY_PCT
    min_decay = math.log(HY_DECAY_TARGET) / HY_SLOW_DECAY_PCT
    return jnp.abs(jnp.linspace(min_decay, max_decay, HY_WIDTH, dtype=F32))[None, :]


def _filt_body(layer_ref, feats_ref, w1_ref, b1_ref, w2_ref, b2_ref, w3_ref, b3_ref, fr_ref,
               wof_ref, wob_ref, dl_ref, c_ref, s_ref, kre_ref, kim_ref, hdn_ref, *, seq):
    del layer_ref
    n = 2 * seq

    @pl.when(pl.program_id(0) == 0)
    def _():
        fr = fr_ref[...]
        hp = functools.partial(jnp.dot, preferred_element_type=F32, precision=HIGHEST)
        h = jnp.sin(fr * (hp(feats_ref[...], w1_ref[...]) + b1_ref[...]))
        h = jnp.sin(fr * (hp(h, w2_ref[...]) + b2_ref[...]))
        hdn_ref[...] = jnp.sin(fr * (hp(h, w3_ref[...]) + b3_ref[...]))

    rowi = lax.broadcasted_iota(jnp.int32, (seq, HY_CT), 0)
    tt = rowi.astype(F32) * (1.0 / (seq - 1))
    window = jnp.exp(-tt * dl_ref[...])
    def split(a):
        hi = a.astype(BF16)
        return hi, (a - hi.astype(F32)).astype(BF16)

    h_hi, h_lo = split(hdn_ref[...])

    def filt(w_ref):
        w_hi, w_lo = split(w_ref[...])
        return _dot(h_hi, w_hi) + (_dot(h_hi, w_lo) + _dot(h_lo, w_hi))

    fwd = filt(wof_ref) * window
    bwd = filt(wob_ref) * window
    bsh = jnp.where(rowi >= 1, pltpu.roll(bwd, 1, axis=0), 0.0)
    rhs = jnp.concatenate([fwd, bsh], axis=1).astype(BF16)
    gc = _dot(c_ref[...], rhs)
    gs = _dot(s_ref[...], rhs)
    alt = (1 - 2 * (rowi & 1)).astype(F32)
    nyq = jnp.sum(alt * (fwd + bsh), axis=0, keepdims=True)
    kre = gc[:, :HY_CT] + gc[:, HY_CT:]
    kim = gs[:, HY_CT:] - gs[:, :HY_CT]
    first = rowi == 0
    kre_ref[...] = kre * jnp.where(first, 1.0 / n, 2.0 / n)
    kim_ref[...] = jnp.where(first, nyq * (1.0 / n), kim * (2.0 / n))


def _filter_spectrum(layer, feats, w1, b1, w2, b2, w3, b3, freq, wout, deltas, ctab, stab):
    seq = feats.shape[0]
    n_oc = HY_ORDER * HY_WIDTH
    tiles = n_oc // HY_CT
    per_w = HY_WIDTH // HY_CT
    sq = lambda: pl.BlockSpec((None, PAD_HID, PAD_HID), lambda j, l: (l[0], 0, 0))
    vec = lambda: pl.BlockSpec((None, 1, PAD_HID), lambda j, l: (l[0], 0, 0))
    grid_spec = pltpu.PrefetchScalarGridSpec(
        num_scalar_prefetch=1,
        grid=(tiles,),
        in_specs=[
            pl.BlockSpec((seq, PAD_HID), lambda j, l: (0, 0)),
            sq(), vec(), sq(), vec(), sq(), vec(), vec(),
            pl.BlockSpec((None, PAD_HID, HY_CT), lambda j, l: (l[0], 0, j)),
            pl.BlockSpec((None, PAD_HID, HY_CT), lambda j, l: (l[0], 0, tiles + j)),
            pl.BlockSpec((1, HY_CT), lambda j, l: (0, j % per_w)),
            _resident((seq, seq), lambda j, l: (0, 0)),
            _resident((seq, seq), lambda j, l: (0, 0)),
        ],
        out_specs=[pl.BlockSpec((None, seq, HY_CT), lambda j, l: (j % per_w, 0, j // per_w)),
                   pl.BlockSpec((None, seq, HY_CT), lambda j, l: (j % per_w, 0, j // per_w))],
        scratch_shapes=[pltpu.VMEM((seq, PAD_HID), F32)],
    )
    return pl.pallas_call(
        functools.partial(_filt_body, seq=seq), grid_spec=grid_spec,
        out_shape=[jax.ShapeDtypeStruct((per_w, seq, HY_ORDER * HY_CT), F32)] * 2,
        compiler_params=_params(1), name="hyena_filter",
    )(layer, feats, w1, b1, w2, b2, w3, b3, freq, wout, wout, deltas, ctab, stab)


def _hyena_body(layer_ref, v_ref, x1_ref, x2_ref, kre_ref, kim_ref, skip_ref, c_ref, s_ref, o_ref,
                z_s, xc_s, xs_s, ya_s, yb_s, *, seq):
    del layer_ref
    rc = HY_ROWS
    n_rc = seq // rc
    alt = (1 - 2 * (lax.broadcasted_iota(jnp.int32, (rc, HY_CT), 0) & 1)).astype(F32)

    def rows(c):
        return pl.ds(pl.multiple_of(c * rc, rc), rc)

    for o, gate_ref in enumerate((x1_ref, x2_ref)):
        z_ref = v_ref if o == 0 else z_s
        last = o == HY_ORDER - 1
        cols = slice(o * HY_CT, (o + 1) * HY_CT)

        def to_bf16(c, acc):
            z = z_ref[rows(c), :]
            ya_s[rows(c), :] = z.astype(BF16)
            yb_s[rows(c), :] = z.astype(BF16)
            return acc + jnp.sum(alt * z, axis=0, keepdims=True)

        xnyq = lax.fori_loop(0, n_rc, to_bf16, jnp.zeros((1, HY_CT), F32))
        xc_s[...] = _dot(c_ref[...], ya_s[...])
        xs_s[...] = _dot(s_ref[...], yb_s[...])

        def spectrum(c, carry):
            xc, xs = xc_s[rows(c), :], xs_s[rows(c), :]
            kre, kim = kre_ref[rows(c), cols], kim_ref[rows(c), cols]
            ya_s[rows(c), :] = (xc * kre + xs * kim).astype(BF16)
            yb_s[rows(c), :] = (xs * kre - xc * kim).astype(BF16)
            return carry

        lax.fori_loop(0, n_rc, spectrum, 0)
        xc_s[...] = _dot(c_ref[...], ya_s[...])
        xs_s[...] = _dot(s_ref[...], yb_s[...])
        ynyq = xnyq * kim_ref[0:1, cols]
        skip = skip_ref[o:o + 1, :]

        def gate(c, carry):
            z = z_ref[rows(c), :]
            y = xc_s[rows(c), :] + xs_s[rows(c), :] + alt * ynyq
            z_new = gate_ref[rows(c), :] * (y + skip * z)
            (o_ref if last else z_s)[rows(c), :] = z_new
            return carry

        lax.fori_loop(0, n_rc, gate, 0)


def _hyena(layer, main, kre, kim, skip, ctab, stab):
    b, seq, _ = main.shape
    tiles = HY_WIDTH // HY_CT
    col0 = (2 * DN_KEY + 2 * DN_VAL) // HY_CT
    hy_spec = lambda part: pl.BlockSpec(
        (None, seq, HY_CT), lambda j, i, l: (i, 0, col0 + part * tiles + j))
    k_spec = lambda: pl.BlockSpec((None, seq, HY_ORDER * HY_CT), lambda j, i, l: (j, 0, 0),
                                  pipeline_mode=pl.Buffered(1))
    grid_spec = pltpu.PrefetchScalarGridSpec(
        num_scalar_prefetch=1,
        grid=(tiles, b),
        in_specs=[
            hy_spec(0), hy_spec(1), hy_spec(2), k_spec(), k_spec(),
            pl.BlockSpec((None, None, HY_ORDER, HY_CT), lambda j, i, l: (l[0], j, 0, 0)),
            _resident((seq, seq), lambda j, i, l: (0, 0)),
            _resident((seq, seq), lambda j, i, l: (0, 0)),
        ],
        out_specs=pl.BlockSpec((None, seq, HY_CT), lambda j, i, l: (i, 0, j)),
        scratch_shapes=[pltpu.VMEM((seq, HY_CT), F32),
                        pltpu.VMEM((seq, HY_CT), F32), pltpu.VMEM((seq, HY_CT), F32),
                        pltpu.VMEM((seq, HY_CT), BF16), pltpu.VMEM((seq, HY_CT), BF16)],
    )
    return pl.pallas_call(
        functools.partial(_hyena_body, seq=seq), grid_spec=grid_spec,
        out_shape=jax.ShapeDtypeStruct((b, seq, HY_WIDTH), F32),
        compiler_params=_params(2), name="hyena",
    )(layer, main, main, main, kre, kim, skip, ctab, stab)


def _mixout_body(layer_ref, x_ref, odn_ref, zz_ref, hn_ref, wo_ref, o_ref):
    del layer_ref
    zz = zz_ref[...]
    ms = jnp.mean(zz * zz, axis=-1, keepdims=True)
    zn = zz * lax.rsqrt(ms + RMS_EPS) * hn_ref[...]
    y = _dot(odn_ref[...].astype(BF16), wo_ref[0:DN_VAL, :])
    y = y + _dot(zn.astype(BF16), wo_ref[DN_VAL:D_MODEL, :])
    o_ref[...] = x_ref[...] + y


def _mixout(layer, x2d, odn2d, zz2d, hy_norm, w_out, *, tm):
    m = x2d.shape[0]
    grid_spec = pltpu.PrefetchScalarGridSpec(
        num_scalar_prefetch=1,
        grid=(m // tm,),
        in_specs=[
            pl.BlockSpec((tm, D_MODEL), lambda i, l: (i, 0)),
            pl.BlockSpec((tm, DN_VAL), lambda i, l: (i, 0)),
            pl.BlockSpec((tm, HY_WIDTH), lambda i, l: (i, 0)),
            pl.BlockSpec((None, 1, HY_WIDTH), lambda i, l: (l[0], 0, 0)),
            pl.BlockSpec((None, D_MODEL, D_MODEL), lambda i, l: (l[0], 0, 0)),
        ],
        out_specs=pl.BlockSpec((tm, D_MODEL), lambda i, l: (i, 0)),
    )
    return pl.pallas_call(
        _mixout_body, grid_spec=grid_spec,
        out_shape=jax.ShapeDtypeStruct((m, D_MODEL), F32),
        compiler_params=_params(1), name="mixout",
    )(layer, x2d, odn2d, zz2d, hy_norm, w_out)


def _final_norm_body(x_ref, w_ref, o_ref):
    x = x_ref[...]
    ms = jnp.mean(x * x, axis=-1, keepdims=True)
    o_ref[...] = x * lax.rsqrt(ms + RMS_EPS) * w_ref[...]


def _final_norm(x2d, w, *, tm):
    m = x2d.shape[0]
    return pl.pallas_call(
        _final_norm_body, grid=(m // tm,),
        in_specs=[pl.BlockSpec((tm, D_MODEL), lambda i: (i, 0)),
                  pl.BlockSpec((1, D_MODEL), lambda i: (0, 0))],
        out_specs=pl.BlockSpec((tm, D_MODEL), lambda i: (i, 0)),
        out_shape=jax.ShapeDtypeStruct((m, D_MODEL), F32),
        compiler_params=_params(1), name="final_norm",
    )(x2d, w)


def _pad_to(a, axis, size):
    pads = [(0, 0)] * a.ndim
    pads[axis] = (0, size - a.shape[axis])
    return jnp.pad(a, pads)


def _row_tile(m):
    return 1024 if m % 1024 == 0 else m


def kernel(x, ffn1_norm, ffn1_w_gate, ffn1_w_up, ffn1_w_down, mix_norm, w_in, dn_conv, dn_a_log, dn_dt_bias, dn_norm, hy_conv, hy_conv_bias, hy_f_w1, hy_f_b1, hy_f_w2, hy_f_b2, hy_f_w3, hy_f_b3, hy_f_freq, hy_f_wout, hy_skip, hy_norm, w_out, ffn2_norm, ffn2_w_gate, ffn2_w_up, ffn2_w_down, final_norm):
    b, seq, _ = x.shape
    m = b * seq
    tm = _row_tile(m)
    n_bd = N_DIRS * DN_HEADS
    bf = lambda a: a.astype(BF16)
    row3 = lambda a: a[:, None, :]

    ba0 = 2 * DN_KEY + 2 * DN_VAL
    w_main = bf(jnp.concatenate([w_in[:, :, :ba0], w_in[:, :, ba0 + 2 * n_bd:]], axis=-1))
    w_ba = bf(_pad_to(w_in[:, :, ba0:ba0 + 2 * n_bd], 2, LANES))
    lane_params = lambda p: _pad_to(
        jnp.pad(p.reshape(DEPTH, 1, n_bd), ((0, 0), (0, 0), (n_bd, 0))), 2, LANES)
    a_log = lane_params(dn_a_log)
    dt_bias = lane_params(dn_dt_bias)
    f_w1 = _pad_to(_pad_to(hy_f_w1, 1, PAD_HID), 2, PAD_HID)
    f_w2 = _pad_to(_pad_to(hy_f_w2, 1, PAD_HID), 2, PAD_HID)
    f_w3 = _pad_to(_pad_to(hy_f_w3, 1, PAD_HID), 2, PAD_HID)
    f_b1, f_b2, f_b3, f_fr = (row3(_pad_to(p, 1, PAD_HID))
                              for p in (hy_f_b1, hy_f_b2, hy_f_b3, hy_f_freq))
    f_wout = _pad_to(hy_f_wout, 1, PAD_HID)
    tiles = HY_WIDTH // HY_CT
    skip = hy_skip.reshape(DEPTH, HY_ORDER, tiles, HY_CT).transpose(0, 2, 1, 3)
    weights = dict(
        ffn1=(row3(ffn1_norm), bf(ffn1_w_gate), bf(ffn1_w_up), bf(ffn1_w_down)),
        ffn2=(row3(ffn2_norm), bf(ffn2_w_gate), bf(ffn2_w_up), bf(ffn2_w_down)),
        w_out=bf(w_out))

    ctab, stab = lax.optimization_barrier(_dft_tables(seq))
    feats = _filter_feats(seq)
    deltas = _filter_deltas()

    def layer_fn(i, carry):
        xc, ctab, stab = carry
        layer = jnp.full((1,), i, jnp.int32)
        x2 = _ffn(layer, xc.reshape(m, D_MODEL), *weights["ffn1"], tm=tm)
        x3 = x2.reshape(b, seq, D_MODEL)
        main, st, gct = _mixin(layer, x3, row3(mix_norm), w_main, w_ba, dn_conv, hy_conv,
                               row3(hy_conv_bias), a_log, dt_bias)
        kre, kim = _filter_spectrum(layer, feats, f_w1, f_b1, f_w2, f_b2, f_w3, f_b3, f_fr,
                                    f_wout, deltas, ctab, stab)
        o_dn = _delta(layer, main, st, gct, row3(dn_norm))
        zz = _hyena(layer, main, kre, kim, skip, ctab, stab)
        x4 = _mixout(layer, x2, o_dn.reshape(m, DN_VAL), zz.reshape(m, HY_WIDTH),
                     row3(hy_norm), weights["w_out"], tm=tm)
        x5 = _ffn(layer, x4, *weights["ffn2"], tm=tm)
        return x5.reshape(b, seq, D_MODEL), ctab, stab

    xf, _, _ = lax.fori_loop(0, DEPTH, layer_fn, (x, ctab, stab))
    return _final_norm(xf.reshape(m, D_MODEL), final_norm[None, :], tm=tm).reshape(b, seq, D_MODEL)
```

```python
import functools
import itertools
import math
import types

import jax
import jax.numpy as jnp
from jax import lax
from jax.experimental import pallas as pl
from jax.experimental.pallas import tpu as pltpu

F32 = jnp.float32
BF16 = jnp.bfloat16
HIGHEST = lax.Precision.HIGHEST

D_MODEL = 1024
DEPTH = 4
N_DIRS = 2
DN_HEADS = 4
DN_DK = 128
DN_DV = 128
DN_KEY = DN_HEADS * DN_DK
DN_VAL = DN_HEADS * DN_DV
DN_CONV = 5
HY_WIDTH = D_MODEL - DN_VAL
HY_ORDER = 2
HY_SHORT = 3
HY_EMB = 33
HY_HID = 64
HY_DIRS = 2
HY_FAST_DECAY_PCT = 0.3
HY_SLOW_DECAY_PCT = 1.5
HY_DECAY_TARGET = 1e-2
D_FF = 2816
RMS_EPS = 1e-6

LANES = 128
SUBLANES = 8
MXU_DIM = 256
VMEM_LIMIT_BYTES = 58 * 1024 * 1024

N_MAIN = 2 * DN_KEY + 2 * DN_VAL + 3 * HY_WIDTH
COL_TILE = 512
N_COL_TILES = N_MAIN // COL_TILE
HY_COL0 = (2 * DN_KEY + 2 * DN_VAL) // COL_TILE
FF_TILE = MXU_DIM
N_FF_TILES = D_FF // FF_TILE
DELTA_CHUNK = MXU_DIM
HEADS_PER_STEP = 2
PAR_CHUNKS = 4
HY_CT = MXU_DIM
HY_ROWS = 256
MIX_ROWS = 1024
PAD_HID = LANES


def _params(n_grid):
    return pltpu.CompilerParams(
        dimension_semantics=("arbitrary",) * n_grid,
        vmem_limit_bytes=VMEM_LIMIT_BYTES)


def _resident(block, imap):
    return pl.BlockSpec(block, imap, pipeline_mode=pl.Buffered(1))


def _dot(a, b):
    return jnp.dot(a, b, preferred_element_type=F32)


def _silu(x):
    return x * jax.nn.sigmoid(x)


def _ffn_body(layer_ref, x_ref, nw_ref, wg_ref, wu_ref, wd_ref, o_ref, h_ref, a_ref):
    del layer_ref
    x = x_ref[...]
    ms = jnp.mean(x * x, axis=-1, keepdims=True)
    h_ref[...] = (x * lax.rsqrt(ms + RMS_EPS) * nw_ref[...]).astype(BF16)
    for f in range(N_FF_TILES):
        cols = slice(f * FF_TILE, (f + 1) * FF_TILE)
        g = _dot(h_ref[...], wg_ref[:, cols])
        u = _dot(h_ref[...], wu_ref[:, cols])
        a_ref[:, cols] = (_silu(g) * u).astype(BF16)
    o_ref[...] = x_ref[...] + 0.5 * _dot(a_ref[...], wd_ref[...])


def _ffn(layer, x2d, norm_w, wg, wu, wd, *, tm):
    m = x2d.shape[0]
    grid_spec = pltpu.PrefetchScalarGridSpec(
        num_scalar_prefetch=1,
        grid=(m // tm,),
        in_specs=[
            pl.BlockSpec((tm, D_MODEL), lambda i, l: (i, 0)),
            pl.BlockSpec((None, 1, D_MODEL), lambda i, l: (l[0], 0, 0)),
            _resident((None, D_MODEL, D_FF), lambda i, l: (l[0], 0, 0)),
            _resident((None, D_MODEL, D_FF), lambda i, l: (l[0], 0, 0)),
            _resident((None, D_FF, D_MODEL), lambda i, l: (l[0], 0, 0)),
        ],
        out_specs=pl.BlockSpec((tm, D_MODEL), lambda i, l: (i, 0)),
        scratch_shapes=[pltpu.VMEM((tm, D_MODEL), BF16), pltpu.VMEM((tm, D_FF), BF16)],
    )
    return pl.pallas_call(
        _ffn_body, grid_spec=grid_spec,
        out_shape=jax.ShapeDtypeStruct((m, D_MODEL), F32),
        compiler_params=_params(1), name="ffn",
    )(layer, x2d, norm_w, wg, wu, wd)


def _softplus(x):
    return jnp.maximum(x, 0.0) + jnp.log1p(jnp.exp(-jnp.abs(x)))


def _mixin_body(layer_ref, x_ref, nw_ref, wm_ref, wba_ref, dnc_ref, hyc_ref, hyb_ref,
                alog_ref, dtb_ref, main_ref, st_ref, gct_ref, h_ref, cbuf0_ref, cbuf1_ref,
                *, seq, chunk):
    del layer_ref
    j = pl.program_id(1)
    n_chunks = seq // chunk
    pad = SUBLANES
    cbufs = (cbuf0_ref, cbuf1_ref)

    def prologue():
        x = x_ref[...]
        ms = jnp.mean(x * x, axis=-1, keepdims=True)
        h_ref[...] = (x * lax.rsqrt(ms + RMS_EPS) * nw_ref[...]).astype(BF16)
        zeros = jnp.zeros((pad, COL_TILE), F32)
        for cbuf in cbufs:
            cbuf[0:pad, :] = zeros
            cbuf[pad + seq:pad + seq + pad, :] = zeros

        ba = _dot(h_ref[...], wba_ref[...])
        lane = lax.broadcasted_iota(jnp.int32, (seq, LANES), 1)
        rowc = lax.broadcasted_iota(jnp.int32, (seq, LANES), 0) & (chunk - 1)
        n_bd = N_DIRS * DN_HEADS
        beta = jax.nn.sigmoid(ba)
        g = -jnp.exp(alog_ref[...]) * _softplus(ba + dtb_ref[...])
        g = jnp.where((lane >= n_bd) & (lane < 2 * n_bd), g, 0.0)
        pre = g
        suf = g
        s = 1
        while s < chunk:
            pre = pre + jnp.where(rowc >= s, pltpu.roll(pre, s, axis=0), 0.0)
            suf = suf + jnp.where(rowc < chunk - s, pltpu.roll(suf, seq - s, axis=0), 0.0)
            s *= 2
        is_fwd = lane < n_bd + DN_HEADS
        gc = jnp.where(is_fwd, pre, suf)
        pre3 = pre.reshape(n_chunks, chunk, LANES)
        suf3 = suf.reshape(n_chunks, chunk, LANES)
        tot_f = jnp.broadcast_to(pre3[:, chunk - 1:chunk, :], pre3.shape).reshape(seq, LANES)
        tot_b = jnp.broadcast_to(suf3[:, 0:1, :], suf3.shape).reshape(seq, LANES)
        gl = jnp.where(is_fwd, tot_f, tot_b)
        st0 = jnp.where(lane < n_bd, beta, gc)
        st_ref[0] = st0
        st_ref[1] = gl - gc
        st_ref[2] = gl
        gct = st0.T
        for c in range(n_chunks):
            gct_ref[c] = gct[n_bd:2 * n_bd, c * chunk:(c + 1) * chunk]

    def conv_rows(slot, r0, w_ref, k, lanes):
        win = cbufs[slot][pl.ds(r0, chunk + 2 * pad), lanes]
        acc = None
        for t in range(k):
            d = t - k // 2
            rolled = win if d == 0 else pltpu.roll(win, (-d) % (chunk + 2 * pad), axis=0)
            term = w_ref[t:t + 1, lanes] * rolled[pad:pad + chunk]
            acc = term if acc is None else acc + term
        return acc

    def l2norm(t):
        return t * lax.rsqrt(jnp.sum(t * t, axis=-1, keepdims=True) + 1e-6)

    def epilogue(tile, r0):
        slot = tile % 2
        for gi in range(COL_TILE // LANES):
            lanes = slice(gi * LANES, (gi + 1) * LANES)
            if tile == 0:
                out = l2norm(_silu(conv_rows(slot, r0, dnc_ref, DN_CONV, lanes))) * (DN_DK ** -0.5)
            elif tile == 1:
                out = l2norm(_silu(conv_rows(slot, r0, dnc_ref, DN_CONV, lanes)))
            elif tile == 2:
                out = _silu(conv_rows(slot, r0, dnc_ref, DN_CONV, lanes))
            elif tile == 3:
                out = _silu(cbufs[slot][pl.ds(r0 + pad, chunk), lanes])
            else:
                out = conv_rows(slot, r0, hyc_ref, HY_SHORT, lanes) + hyb_ref[:, lanes]
            main_ref[pl.ds(r0, chunk), lanes] = out

    blk = min(MIX_ROWS, seq)

    def row_blocks(project_tile, finish_tile):
        def body(c, carry):
            r0 = pl.multiple_of(c * blk, blk)
            if project_tile is not None:
                cbufs[project_tile % 2][pl.ds(pad + r0, blk), :] = _dot(
                    h_ref[pl.ds(r0, blk), :], wm_ref[...])
            if finish_tile is not None:
                for i in range(blk // chunk):
                    epilogue(finish_tile, r0 + i * chunk)
            return carry
        lax.fori_loop(0, seq // blk, body, 0)

    @pl.when(j == 0)
    def _():
        prologue()
        row_blocks(0, None)

    for t in range(1, N_COL_TILES):
        @pl.when(j == t)
        def _(t=t):
            row_blocks(t, t - 1)

    @pl.when(j == N_COL_TILES)
    def _():
        row_blocks(None, N_COL_TILES - 1)


def _mixin(layer, x, norm_w, w_main, w_ba, dn_conv, hy_conv, hy_bias, a_log, dt_bias):
    b, seq, _ = x.shape
    chunk = DELTA_CHUNK
    n_chunks = seq // chunk
    n_bd = N_DIRS * DN_HEADS
    dn_tiles = (2 * DN_KEY + DN_VAL) // COL_TILE
    hy_tiles = 3 * HY_WIDTH // COL_TILE
    done = lambda j: jnp.maximum(j - 1, 0)
    grid_spec = pltpu.PrefetchScalarGridSpec(
        num_scalar_prefetch=1,
        grid=(b, N_COL_TILES + 1),
        in_specs=[
            pl.BlockSpec((None, seq, D_MODEL), lambda i, j, l: (i, 0, 0)),
            pl.BlockSpec((None, 1, D_MODEL), lambda i, j, l: (l[0], 0, 0)),
            pl.BlockSpec((None, D_MODEL, COL_TILE),
                         lambda i, j, l: (l[0], 0, jnp.minimum(j, N_COL_TILES - 1))),
            pl.BlockSpec((None, D_MODEL, LANES), lambda i, j, l: (l[0], 0, 0)),
            pl.BlockSpec((None, DN_CONV, COL_TILE),
                         lambda i, j, l: (l[0], 0, jnp.minimum(done(j), dn_tiles - 1))),
            pl.BlockSpec((None, HY_SHORT, COL_TILE),
                         lambda i, j, l: (l[0], 0, jnp.clip(done(j) - HY_COL0, 0, hy_tiles - 1))),
            pl.BlockSpec((None, 1, COL_TILE),
                         lambda i, j, l: (l[0], 0, jnp.clip(done(j) - HY_COL0, 0, hy_tiles - 1))),
            pl.BlockSpec((None, 1, LANES), lambda i, j, l: (l[0], 0, 0)),
            pl.BlockSpec((None, 1, LANES), lambda i, j, l: (l[0], 0, 0)),
        ],
        out_specs=[
            pl.BlockSpec((None, seq, COL_TILE), lambda i, j, l: (i, 0, done(j))),
            pl.BlockSpec((None, 3, seq, LANES), lambda i, j, l: (i, 0, 0, 0)),
            pl.BlockSpec((None, n_chunks, n_bd, chunk), lambda i, j, l: (i, 0, 0, 0)),
        ],
        scratch_shapes=[pltpu.VMEM((seq, D_MODEL), BF16),
                        pltpu.VMEM((seq + 2 * SUBLANES, COL_TILE), F32),
                        pltpu.VMEM((seq + 2 * SUBLANES, COL_TILE), F32)],
    )
    return pl.pallas_call(
        functools.partial(_mixin_body, seq=seq, chunk=chunk), grid_spec=grid_spec,
        out_shape=[jax.ShapeDtypeStruct((b, seq, N_MAIN), F32),
                   jax.ShapeDtypeStruct((b, 3, seq, LANES), F32),
                   jax.ShapeDtypeStruct((b, n_chunks, n_bd, chunk), F32)],
        compiler_params=_params(2), name="mixin",
    )(layer, x, norm_w, w_main, w_ba, dn_conv, hy_conv, hy_bias, a_log, dt_bias)


_INV_LEVELS = (8, 16, 32, 64, 128)


def _delta_body(layer_ref, q_ref, k_ref, v_ref, z_ref, st_ref, gct_ref, nw_ref, o_ref,
                wq_s, u_s, att_s, kdt_s, gam_s, s_s, oacc_s, *, seq):
    del layer_ref
    hp = pl.program_id(1)
    heads = range(HEADS_PER_STEP)
    c_ = DELTA_CHUNK
    n_chunks = seq // c_
    n_bd = N_DIRS * DN_HEADS
    row = lax.broadcasted_iota(jnp.int32, (c_, c_), 0)
    col = lax.broadcasted_iota(jnp.int32, (c_, c_), 1)
    lane = lax.broadcasted_iota(jnp.int32, (c_, LANES), 1)

    half_row = lax.broadcasted_iota(jnp.int32, (c_ // 2, c_), 0)
    half_col = lax.broadcasted_iota(jnp.int32, (c_ // 2, c_), 1)

    def pick(x, idx):
        return jnp.sum(jnp.where(lane == idx, x, 0.0), axis=1, keepdims=True)

    par = math.gcd(max(PAR_CHUNKS // HEADS_PER_STEP, 1), n_chunks)

    def phase_a(it, carry):
        chains, cells = [], []
        for hh, j in itertools.product(heads, range(par)):
            hd = hp * HEADS_PER_STEP + hh
            head_lanes = slice(hh * DN_DK, (hh + 1) * DN_DK)
            g = it * par + j
            r0 = pl.multiple_of(g * c_, c_)
            q = q_ref[pl.ds(r0, c_), head_lanes]
            k = k_ref[pl.ds(r0, c_), head_lanes]
            v = v_ref[pl.ds(r0, c_), head_lanes]
            st0 = st_ref[0, pl.ds(r0, c_), :]
            st1 = st_ref[1, pl.ds(r0, c_), :]
            st2 = st_ref[2, pl.ds(r0, c_), :]
            cell = types.SimpleNamespace(k_bf=k.astype(BF16), kb_bf=[], chains=[])
            cells.append(cell)
            for d in range(N_DIRS):
                ch = types.SimpleNamespace(d=d, step=g if d == 0 else n_chunks - 1 - g, hh=hh,
                                           lanes=slice(d * DN_DV, (d + 1) * DN_DV))
                cell.chains.append(ch)
                bi = d * DN_HEADS + hd
                beta = pick(st0, bi)
                gc = pick(st0, n_bd + bi)
                egc = jnp.exp(gc)
                ekd = jnp.exp(pick(st1, n_bd + bi))
                egl = jnp.exp(pick(st2, n_bd + bi))
                gcr = gct_ref[g, pl.ds(bi, 1), :]
                incl = (row >= col) if d == 0 else (row <= col)
                ch.strict = (row > col) if d == 0 else (row < col)
                ch.dec = jnp.where(incl, jnp.exp(jnp.where(incl, gc - gcr, 0.0)), 0.0)
                kb = k * beta
                cell.kb_bf.append(kb.astype(BF16))
                ch.rhs = jnp.concatenate([(v * beta).astype(BF16), (kb * egc).astype(BF16)], axis=1)
                wq_s[hh, ch.step, c_:2 * c_, ch.lanes] = (q * egc).astype(BF16)
                kdt_s[hh, ch.step, d * DN_DK:(d + 1) * DN_DK, :] = (k * ekd).T.astype(BF16)
                gam_s[hh, ch.step, d * DN_DK:(d + 1) * DN_DK, :] = jnp.broadcast_to(
                    egl[0:DN_DK, :], (DN_DK, N_DIRS * DN_DV))
                chains.append(ch)
            cell.lhs = jnp.concatenate(cell.kb_bf + [q.astype(BF16)], axis=0)

        for cell in cells:
            s1 = lax.dot_general(cell.lhs, cell.k_bf, (((1,), (1,)), ((), ())),
                                 preferred_element_type=F32)
            for ch in cell.chains:
                ch.kk = s1[ch.d * c_:(ch.d + 1) * c_]
                ch.qk = s1[N_DIRS * c_:]
        for ch in chains:
            a = jnp.where(ch.strict, ch.kk * ch.dec, 0.0)
            att_s[ch.hh, ch.step, :, ch.d * c_:(ch.d + 1) * c_] = (ch.qk * ch.dec).astype(BF16)
            ch.a = a
            ch.b0 = jnp.where((row >> 3) == (col >> 3), -a, 0.0)
            ch.b0_bf = ch.b0.astype(BF16)
        for ch in chains:
            ch.r1 = _dot(ch.b0_bf, ch.b0_bf)
        for ch in chains:
            r1_bf = ch.r1.astype(BF16)
            ch.m = _dot(jnp.concatenate([ch.b0_bf, r1_bf], axis=0), r1_bf)
        for ch in chains:
            ch.q2 = ch.b0 + ch.r1 + ch.m[:c_]
            ch.r2 = ch.m[c_:]
            ch.q2r2 = _dot(ch.q2.astype(BF16), ch.r2.astype(BF16))
        for ch in chains:
            ch.xf = jnp.where(row == col, 1.0, ch.q2 + ch.r2 + ch.q2r2)
            ch.x = ch.xf.astype(BF16)
        for s in _INV_LEVELS:
            sh = s.bit_length() - 1
            n_pairs = c_ // (2 * s)

            def take(m, odd, s=s, n_pairs=n_pairs):
                return jnp.concatenate(
                    [m[(2 * i + odd) * s:(2 * i + odd + 1) * s] for i in range(n_pairs)], axis=0)

            def interleave(even, odd, s=s, n_pairs=n_pairs):
                pieces = []
                for i in range(n_pairs):
                    pieces += [even[i * s:(i + 1) * s], odd[i * s:(i + 1) * s]]
                return jnp.concatenate(pieces, axis=0)

            for ch in chains:
                act = 1 - ch.d
                partner_cols = (half_col >> sh) == (((half_row >> sh) << 1) + ch.d)
                e = jnp.where(partner_cols, take(ch.a, act), 0.0)
                ch.m1 = _dot(e.astype(BF16), ch.x)
            for ch in chains:
                act = 1 - ch.d
                zeros = jnp.zeros_like(ch.m1)
                m1_rows = interleave(zeros, -ch.m1) if act else interleave(-ch.m1, zeros)
                ch.m2 = _dot(take(ch.xf, act).astype(BF16), m1_rows.astype(BF16))
            for ch in chains:
                act = 1 - ch.d
                new = take(ch.xf, act) + ch.m2
                keep = take(ch.xf, ch.d)
                ch.xf = interleave(keep, new) if act else interleave(new, keep)
                ch.x = ch.xf.astype(BF16)
        for ch in chains:
            ch.uw = _dot(ch.x, ch.rhs)
        for ch in chains:
            u_s[ch.hh, ch.step, :, ch.lanes] = ch.uw[:, :DN_DV]
            wq_s[ch.hh, ch.step, 0:c_, ch.lanes] = ch.uw[:, DN_DV:].astype(BF16)
        return carry

    lax.fori_loop(0, n_chunks // par, phase_a, 0)

    s_s[...] = jnp.zeros_like(s_s)
    oacc_s[...] = jnp.zeros_like(oacc_s)
    w2 = N_DIRS * DN_DV
    lane2 = lax.broadcasted_iota(jnp.int32, (c_, w2), 1)
    blockdiag = (lax.broadcasted_iota(jnp.int32, (w2, w2), 0) >> 7) == (
        lax.broadcasted_iota(jnp.int32, (w2, w2), 1) >> 7)

    def scan_step(s, carry):
        rf = pl.multiple_of(s * c_, c_)
        rb = pl.multiple_of((n_chunks - 1 - s) * c_, c_)
        wqs = [_dot(wq_s[hh, s], s_s[hh].astype(BF16)) for hh in heads]
        vn = [u_s[hh, s] - wqs[hh][:c_] for hh in heads]
        kv = [_dot(kdt_s[hh, s], vn[hh].astype(BF16)) for hh in heads]
        for hh in heads:
            s_s[hh] = s_s[hh] * gam_s[hh, s] + jnp.where(blockdiag, kv[hh], 0.0)
        for hh in heads:
            rhs = jnp.concatenate([jnp.where(lane2 < DN_DV, vn[hh], 0.0).astype(BF16),
                                   jnp.where(lane2 >= DN_DV, vn[hh], 0.0).astype(BF16)], axis=0)
            o = wqs[hh][c_:] + _dot(att_s[hh, s], rhs)
            oacc_s[hh, pl.ds(rf, c_), :] += o[:, :DN_DV]
            oacc_s[hh, pl.ds(rb, c_), :] += o[:, DN_DV:]
        return carry

    lax.fori_loop(0, n_chunks, scan_step, 0)

    for hh in heads:
        head_lanes = slice(hh * DN_DV, (hh + 1) * DN_DV)
        o = oacc_s[hh]
        ms = jnp.mean(o * o, axis=-1, keepdims=True)
        o_ref[:, head_lanes] = o * lax.rsqrt(ms + RMS_EPS) * nw_ref[...] * z_ref[:, head_lanes]


def _delta(layer, main, st, gct, norm_w):
    b, seq, _ = main.shape
    c_ = DELTA_CHUNK
    n_chunks = seq // c_
    n_bd = N_DIRS * DN_HEADS
    w2 = N_DIRS * DN_DV
    hps = HEADS_PER_STEP
    groups = DN_HEADS // hps
    head_spec = lambda part: pl.BlockSpec(
        (None, seq, hps * DN_DK), lambda i, h, l: (i, 0, part * groups + h))
    grid_spec = pltpu.PrefetchScalarGridSpec(
        num_scalar_prefetch=1,
        grid=(b, groups),
        in_specs=[
            head_spec(0), head_spec(1), head_spec(2), head_spec(3),
            pl.BlockSpec((None, 3, seq, LANES), lambda i, h, l: (i, 0, 0, 0)),
            pl.BlockSpec((None, n_chunks, n_bd, c_), lambda i, h, l: (i, 0, 0, 0)),
            pl.BlockSpec((None, 1, DN_DV), lambda i, h, l: (l[0], 0, 0)),
        ],
        out_specs=pl.BlockSpec((None, seq, hps * DN_DV), lambda i, h, l: (i, 0, h)),
        scratch_shapes=[
            pltpu.VMEM((hps, n_chunks, 2 * c_, w2), BF16),
            pltpu.VMEM((hps, n_chunks, c_, w2), F32),
            pltpu.VMEM((hps, n_chunks, c_, N_DIRS * c_), BF16),
            pltpu.VMEM((hps, n_chunks, N_DIRS * DN_DK, c_), BF16),
            pltpu.VMEM((hps, n_chunks, N_DIRS * DN_DK, w2), F32),
            pltpu.VMEM((hps, N_DIRS * DN_DK, w2), F32),
            pltpu.VMEM((hps, seq, DN_DV), F32),
        ],
    )
    return pl.pallas_call(
        functools.partial(_delta_body, seq=seq), grid_spec=grid_spec,
        out_shape=jax.ShapeDtypeStruct((b, seq, DN_VAL), F32),
        compiler_params=_params(2), name="delta",
    )(layer, main, main, main, main, st, gct, norm_w)


def _dft_tables(seq):
    n = 2 * seq
    k = lax.broadcasted_iota(jnp.int32, (seq, seq), 0)
    t = lax.broadcasted_iota(jnp.int32, (seq, seq), 1)
    ang = ((k * t) & (n - 1)).astype(F32) * (2.0 * math.pi / n)
    return jnp.cos(ang).astype(BF16), jnp.sin(ang).astype(BF16)


def _filter_feats(seq):
    t = jnp.linspace(0.0, 1.0, seq, dtype=F32)[:, None]
    bands = (HY_EMB - 1) // 2
    ang = ((2.0 * math.pi / seq) * jnp.arange(seq, dtype=F32)[:, None]
           * jnp.linspace(1e-4, bands - 1, bands, dtype=F32)[None, :])
    feats = jnp.concatenate([t, jnp.cos(ang), -jnp.sin(ang)], axis=-1)
    return jnp.pad(feats, ((0, 0), (0, PAD_HID - HY_EMB)))


def _filter_deltas():
    max_decay = math.log(HY_DECAY_TARGET) / HY_FAST_DECAY_PCT
    min_decay = math.log(HY_DECAY_TARGET) / HY_SLOW_DECAY_PCT
    return jnp.abs(jnp.linspace(min_decay, max_decay, HY_WIDTH, dtype=F32))[None, :]


def _filt_body(layer_ref, feats_ref, w1_ref, b1_ref, w2_ref, b2_ref, w3_ref, b3_ref, fr_ref,
               wof_ref, wob_ref, dl_ref, c_ref, s_ref, kre_ref, kim_ref,
               hdn_ref, fwd_s, bwd_s, rhs_s, gc_s, gs_s, *, seq):
    del layer_ref
    n = 2 * seq

    @pl.when(pl.program_id(0) == 0)
    def _():
        fr = fr_ref[...]
        hp = functools.partial(jnp.dot, preferred_element_type=F32, precision=HIGHEST)
        h = jnp.sin(fr * (hp(feats_ref[...], w1_ref[...]) + b1_ref[...]))
        h = jnp.sin(fr * (hp(h, w2_ref[...]) + b2_ref[...]))
        hdn_ref[...] = jnp.sin(fr * (hp(h, w3_ref[...]) + b3_ref[...]))

    rc = HY_ROWS
    n_rc = seq // rc
    pad = SUBLANES
    rowi = lax.broadcasted_iota(jnp.int32, (rc, HY_CT), 0)
    alt = (1 - 2 * (rowi & 1)).astype(F32)

    def rows(c, extra=0):
        return pl.ds(pl.multiple_of(c * rc, rc), rc + extra)

    def split(a):
        hi = a.astype(BF16)
        return hi, (a - hi.astype(F32)).astype(BF16)

    wf_hi, wf_lo = split(wof_ref[...])
    wb_hi, wb_lo = split(wob_ref[...])
    bwd_s[0:pad, :] = jnp.zeros((pad, HY_CT), F32)

    def taps(c, carry):
        h_hi, h_lo = split(hdn_ref[rows(c), :])
        tt = (rowi + c * rc).astype(F32) * (1.0 / (seq - 1))
        window = jnp.exp(-tt * dl_ref[...])
        fwd = (_dot(h_hi, wf_hi) + (_dot(h_hi, wf_lo) + _dot(h_lo, wf_hi))) * window
        bwd = (_dot(h_hi, wb_hi) + (_dot(h_hi, wb_lo) + _dot(h_lo, wb_hi))) * window
        fwd_s[rows(c), :] = fwd
        bwd_s[pl.ds(pl.multiple_of(c * rc, rc) + pad, rc), :] = bwd
        return carry

    lax.fori_loop(0, n_rc, taps, 0)

    def pack(c, nyq):
        bsh = pltpu.roll(bwd_s[rows(c, pad), :], 1, axis=0)[pad:pad + rc]
        fwd = fwd_s[rows(c), :]
        rhs_s[rows(c), 0:HY_CT] = fwd.astype(BF16)
        rhs_s[rows(c), HY_CT:2 * HY_CT] = bsh.astype(BF16)
        return nyq + jnp.sum(alt * (fwd + bsh), axis=0, keepdims=True)

    nyq = lax.fori_loop(0, n_rc, pack, jnp.zeros((1, HY_CT), F32))
    gc_s[...] = _dot(c_ref[...], rhs_s[...])
    gs_s[...] = _dot(s_ref[...], rhs_s[...])

    def combine(c, carry):
        gc = gc_s[rows(c), :]
        gs = gs_s[rows(c), :]
        first = (rowi + c * rc) == 0
        kre = gc[:, :HY_CT] + gc[:, HY_CT:]
        kim = gs[:, HY_CT:] - gs[:, :HY_CT]
        kre_ref[rows(c), :] = kre * jnp.where(first, 1.0 / n, 2.0 / n)
        kim_ref[rows(c), :] = jnp.where(first, nyq * (1.0 / n), kim * (2.0 / n))
        return carry

    lax.fori_loop(0, n_rc, combine, 0)


def _filter_spectrum(layer, feats, w1, b1, w2, b2, w3, b3, freq, wout, deltas, ctab, stab):
    seq = feats.shape[0]
    n_oc = HY_ORDER * HY_WIDTH
    tiles = n_oc // HY_CT
    per_w = HY_WIDTH // HY_CT
    sq = lambda: pl.BlockSpec((None, PAD_HID, PAD_HID), lambda j, l: (l[0], 0, 0))
    vec = lambda: pl.BlockSpec((None, 1, PAD_HID), lambda j, l: (l[0], 0, 0))
    grid_spec = pltpu.PrefetchScalarGridSpec(
        num_scalar_prefetch=1,
        grid=(tiles,),
        in_specs=[
            pl.BlockSpec((seq, PAD_HID), lambda j, l: (0, 0)),
            sq(), vec(), sq(), vec(), sq(), vec(), vec(),
            pl.BlockSpec((None, PAD_HID, HY_CT), lambda j, l: (l[0], 0, j)),
            pl.BlockSpec((None, PAD_HID, HY_CT), lambda j, l: (l[0], 0, tiles + j)),
            pl.BlockSpec((1, HY_CT), lambda j, l: (0, j % per_w)),
            _resident((seq, seq), lambda j, l: (0, 0)),
            _resident((seq, seq), lambda j, l: (0, 0)),
        ],
        out_specs=[pl.BlockSpec((None, seq, HY_CT), lambda j, l: (j % per_w, 0, j // per_w)),
                   pl.BlockSpec((None, seq, HY_CT), lambda j, l: (j % per_w, 0, j // per_w))],
        scratch_shapes=[pltpu.VMEM((seq, PAD_HID), F32),
                        pltpu.VMEM((seq, HY_CT), F32),
                        pltpu.VMEM((seq + SUBLANES, HY_CT), F32),
                        pltpu.VMEM((seq, 2 * HY_CT), BF16),
                        pltpu.VMEM((seq, 2 * HY_CT), F32),
                        pltpu.VMEM((seq, 2 * HY_CT), F32)],
    )
    return pl.pallas_call(
        functools.partial(_filt_body, seq=seq), grid_spec=grid_spec,
        out_shape=[jax.ShapeDtypeStruct((per_w, seq, HY_ORDER * HY_CT), F32)] * 2,
        compiler_params=_params(1), name="hyena_filter",
    )(layer, feats, w1, b1, w2, b2, w3, b3, freq, wout, wout, deltas, ctab, stab)


def _hyena_body(layer_ref, v_ref, x1_ref, x2_ref, kre_ref, kim_ref, skip_ref, c_ref, s_ref, o_ref,
                z_s, xc_s, xs_s, ya_s, yb_s, *, seq):
    del layer_ref
    rc = HY_ROWS
    n_rc = seq // rc
    alt = (1 - 2 * (lax.broadcasted_iota(jnp.int32, (rc, HY_CT), 0) & 1)).astype(F32)

    def rows(c):
        return pl.ds(pl.multiple_of(c * rc, rc), rc)

    def stage(c, z):
        ya_s[rows(c), :] = z.astype(BF16)
        yb_s[rows(c), :] = z.astype(BF16)
        return jnp.sum(alt * z, axis=0, keepdims=True)

    xnyq = lax.fori_loop(0, n_rc, lambda c, acc: acc + stage(c, v_ref[rows(c), :]),
                         jnp.zeros((1, HY_CT), F32))

    for o, gate_ref in enumerate((x1_ref, x2_ref)):
        z_ref = v_ref if o == 0 else z_s
        last = o == HY_ORDER - 1
        cols = slice(o * HY_CT, (o + 1) * HY_CT)
        xc_s[...] = _dot(c_ref[...], ya_s[...])
        xs_s[...] = _dot(s_ref[...], yb_s[...])

        def spectrum(c, carry):
            xc, xs = xc_s[rows(c), :], xs_s[rows(c), :]
            kre, kim = kre_ref[rows(c), cols], kim_ref[rows(c), cols]
            ya_s[rows(c), :] = (xc * kre + xs * kim).astype(BF16)
            yb_s[rows(c), :] = (xs * kre - xc * kim).astype(BF16)
            return carry

        lax.fori_loop(0, n_rc, spectrum, 0)
        xc_s[...] = _dot(c_ref[...], ya_s[...])
        xs_s[...] = _dot(s_ref[...], yb_s[...])
        ynyq = xnyq * kim_ref[0:1, cols]
        skip = skip_ref[o:o + 1, :]

        def gate(c, acc):
            z = z_ref[rows(c), :]
            y = xc_s[rows(c), :] + xs_s[rows(c), :] + alt * ynyq
            z_new = gate_ref[rows(c), :] * (y + skip * z)
            if last:
                o_ref[rows(c), :] = z_new
                return acc
            z_s[rows(c), :] = z_new
            return acc + stage(c, z_new)

        xnyq = lax.fori_loop(0, n_rc, gate, jnp.zeros((1, HY_CT), F32))


def _hyena(layer, main, kre, kim, skip, ctab, stab):
    b, seq, _ = main.shape
    tiles = HY_WIDTH // HY_CT
    col0 = (2 * DN_KEY + 2 * DN_VAL) // HY_CT
    hy_spec = lambda part: pl.BlockSpec(
        (None, seq, HY_CT), lambda j, i, l: (i, 0, col0 + part * tiles + j))
    k_spec = lambda: pl.BlockSpec((None, seq, HY_ORDER * HY_CT), lambda j, i, l: (j, 0, 0),
                                  pipeline_mode=pl.Buffered(1))
    grid_spec = pltpu.PrefetchScalarGridSpec(
        num_scalar_prefetch=1,
        grid=(tiles, b),
        in_specs=[
            hy_spec(0), hy_spec(1), hy_spec(2), k_spec(), k_spec(),
            pl.BlockSpec((None, None, HY_ORDER, HY_CT), lambda j, i, l: (l[0], j, 0, 0)),
            _resident((seq, seq), lambda j, i, l: (0, 0)),
            _resident((seq, seq), lambda j, i, l: (0, 0)),
        ],
        out_specs=pl.BlockSpec((None, seq, HY_CT), lambda j, i, l: (i, 0, j)),
        scratch_shapes=[pltpu.VMEM((seq, HY_CT), F32),
                        pltpu.VMEM((seq, HY_CT), F32), pltpu.VMEM((seq, HY_CT), F32),
                        pltpu.VMEM((seq, HY_CT), BF16), pltpu.VMEM((seq, HY_CT), BF16)],
    )
    return pl.pallas_call(
        functools.partial(_hyena_body, seq=seq), grid_spec=grid_spec,
        out_shape=jax.ShapeDtypeStruct((b, seq, HY_WIDTH), F32),
        compiler_params=_params(2), name="hyena",
    )(layer, main, main, main, kre, kim, skip, ctab, stab)


def _mixout_body(layer_ref, x_ref, odn_ref, zz_ref, hn_ref, wo_ref, o_ref):
    del layer_ref
    zz = zz_ref[...]
    ms = jnp.mean(zz * zz, axis=-1, keepdims=True)
    zn = zz * lax.rsqrt(ms + RMS_EPS) * hn_ref[...]
    y = _dot(odn_ref[...].astype(BF16), wo_ref[0:DN_VAL, :])
    y = y + _dot(zn.astype(BF16), wo_ref[DN_VAL:D_MODEL, :])
    o_ref[...] = x_ref[...] + y


def _mixout(layer, x2d, odn2d, zz2d, hy_norm, w_out, *, tm):
    m = x2d.shape[0]
    grid_spec = pltpu.PrefetchScalarGridSpec(
        num_scalar_prefetch=1,
        grid=(m // tm,),
        in_specs=[
            pl.BlockSpec((tm, D_MODEL), lambda i, l: (i, 0)),
            pl.BlockSpec((tm, DN_VAL), lambda i, l: (i, 0)),
            pl.BlockSpec((tm, HY_WIDTH), lambda i, l: (i, 0)),
            pl.BlockSpec((None, 1, HY_WIDTH), lambda i, l: (l[0], 0, 0)),
            pl.BlockSpec((None, D_MODEL, D_MODEL), lambda i, l: (l[0], 0, 0)),
        ],
        out_specs=pl.BlockSpec((tm, D_MODEL), lambda i, l: (i, 0)),
    )
    return pl.pallas_call(
        _mixout_body, grid_spec=grid_spec,
        out_shape=jax.ShapeDtypeStruct((m, D_MODEL), F32),
        compiler_params=_params(1), name="mixout",
    )(layer, x2d, odn2d, zz2d, hy_norm, w_out)


def _final_norm_body(x_ref, w_ref, o_ref):
    x = x_ref[...]
    ms = jnp.mean(x * x, axis=-1, keepdims=True)
    o_ref[...] = x * lax.rsqrt(ms + RMS_EPS) * w_ref[...]


def _final_norm(x2d, w, *, tm):
    m = x2d.shape[0]
    return pl.pallas_call(
        _final_norm_body, grid=(m // tm,),
        in_specs=[pl.BlockSpec((tm, D_MODEL), lambda i: (i, 0)),
                  pl.BlockSpec((1, D_MODEL), lambda i: (0, 0))],
        out_specs=pl.BlockSpec((tm, D_MODEL), lambda i: (i, 0)),
        out_shape=jax.ShapeDtypeStruct((m, D_MODEL), F32),
        compiler_params=_params(1), name="final_norm",
    )(x2d, w)


def _pad_to(a, axis, size):
    pads = [(0, 0)] * a.ndim
    pads[axis] = (0, size - a.shape[axis])
    return jnp.pad(a, pads)


def _row_tile(m):
    return 1024 if m % 1024 == 0 else m


def kernel(x, ffn1_norm, ffn1_w_gate, ffn1_w_up, ffn1_w_down, mix_norm, w_in, dn_conv, dn_a_log, dn_dt_bias, dn_norm, hy_conv, hy_conv_bias, hy_f_w1, hy_f_b1, hy_f_w2, hy_f_b2, hy_f_w3, hy_f_b3, hy_f_freq, hy_f_wout, hy_skip, hy_norm, w_out, ffn2_norm, ffn2_w_gate, ffn2_w_up, ffn2_w_down, final_norm):
    b, seq, _ = x.shape
    m = b * seq
    tm = _row_tile(m)
    n_bd = N_DIRS * DN_HEADS
    bf = lambda a: a.astype(BF16)
    row3 = lambda a: a[:, None, :]

    ba0 = 2 * DN_KEY + 2 * DN_VAL
    w_main = bf(jnp.concatenate([w_in[:, :, :ba0], w_in[:, :, ba0 + 2 * n_bd:]], axis=-1))
    w_ba = bf(_pad_to(w_in[:, :, ba0:ba0 + 2 * n_bd], 2, LANES))
    lane_params = lambda p: _pad_to(
        jnp.pad(p.reshape(DEPTH, 1, n_bd), ((0, 0), (0, 0), (n_bd, 0))), 2, LANES)
    a_log = lane_params(dn_a_log)
    dt_bias = lane_params(dn_dt_bias)
    f_w1 = _pad_to(_pad_to(hy_f_w1, 1, PAD_HID), 2, PAD_HID)
    f_w2 = _pad_to(_pad_to(hy_f_w2, 1, PAD_HID), 2, PAD_HID)
    f_w3 = _pad_to(_pad_to(hy_f_w3, 1, PAD_HID), 2, PAD_HID)
    f_b1, f_b2, f_b3, f_fr = (row3(_pad_to(p, 1, PAD_HID))
                              for p in (hy_f_b1, hy_f_b2, hy_f_b3, hy_f_freq))
    f_wout = _pad_to(hy_f_wout, 1, PAD_HID)
    tiles = HY_WIDTH // HY_CT
    skip = hy_skip.reshape(DEPTH, HY_ORDER, tiles, HY_CT).transpose(0, 2, 1, 3)
    weights = dict(
        ffn1=(row3(ffn1_norm), bf(ffn1_w_gate), bf(ffn1_w_up), bf(ffn1_w_down)),
        ffn2=(row3(ffn2_norm), bf(ffn2_w_gate), bf(ffn2_w_up), bf(ffn2_w_down)),
        w_out=bf(w_out))

    ctab, stab = lax.optimization_barrier(_dft_tables(seq))
    feats = _filter_feats(seq)
    deltas = _filter_deltas()

    def layer_fn(i, carry):
        xc, ctab, stab = carry
        layer = jnp.full((1,), i, jnp.int32)
        x2 = _ffn(layer, xc.reshape(m, D_MODEL), *weights["ffn1"], tm=tm)
        x3 = x2.reshape(b, seq, D_MODEL)
        main, st, gct = _mixin(layer, x3, row3(mix_norm), w_main, w_ba, dn_conv, hy_conv,
                               row3(hy_conv_bias), a_log, dt_bias)
        kre, kim = _filter_spectrum(layer, feats, f_w1, f_b1, f_w2, f_b2, f_w3, f_b3, f_fr,
                                    f_wout, deltas, ctab, stab)
        o_dn = _delta(layer, main, st, gct, row3(dn_norm))
        zz = _hyena(layer, main, kre, kim, skip, ctab, stab)
        x4 = _mixout(layer, x2, o_dn.reshape(m, DN_VAL), zz.reshape(m, HY_WIDTH),
                     row3(hy_norm), weights["w_out"], tm=tm)
        x5 = _ffn(layer, x4, *weights["ffn2"], tm=tm)
        return x5.reshape(b, seq, D_MODEL), ctab, stab

    xf, _, _ = lax.fori_loop(0, DEPTH, layer_fn, (x, ctab, stab))
    return _final_norm(xf.reshape(m, D_MODEL), final_norm[None, :], tm=tm).reshape(b, seq, D_MODEL)
```

```python
import functools
import itertools
import math
import types

import jax
import jax.numpy as jnp
from jax import lax
from jax.experimental import pallas as pl
from jax.experimental.pallas import tpu as pltpu

F32 = jnp.float32
BF16 = jnp.bfloat16
HIGHEST = lax.Precision.HIGHEST

D_MODEL = 1024
DEPTH = 4
N_DIRS = 2
DN_HEADS = 4
DN_DK = 128
DN_DV = 128
DN_KEY = DN_HEADS * DN_DK
DN_VAL = DN_HEADS * DN_DV
DN_CONV = 5
HY_WIDTH = D_MODEL - DN_VAL
HY_ORDER = 2
HY_SHORT = 3
HY_EMB = 33
HY_HID = 64
HY_DIRS = 2
HY_FAST_DECAY_PCT = 0.3
HY_SLOW_DECAY_PCT = 1.5
HY_DECAY_TARGET = 1e-2
D_FF = 2816
RMS_EPS = 1e-6

LANES = 128
SUBLANES = 8
MXU_DIM = 256
VMEM_LIMIT_BYTES = 58 * 1024 * 1024

N_MAIN = 2 * DN_KEY + 2 * DN_VAL + 3 * HY_WIDTH
COL_TILE = 512
N_COL_TILES = N_MAIN // COL_TILE
HY_COL0 = (2 * DN_KEY + 2 * DN_VAL) // COL_TILE
FF_TILE = MXU_DIM
N_FF_TILES = D_FF // FF_TILE
DELTA_CHUNK = MXU_DIM
HEADS_PER_STEP = 2
PAR_CHUNKS = 4
HY_CT = MXU_DIM
HY_ROWS = 256
PAD_HID = LANES


def _params(n_grid):
    return pltpu.CompilerParams(
        dimension_semantics=("arbitrary",) * n_grid,
        vmem_limit_bytes=VMEM_LIMIT_BYTES)


def _resident(block, imap):
    return pl.BlockSpec(block, imap, pipeline_mode=pl.Buffered(1))


def _dot(a, b):
    return jnp.dot(a, b, preferred_element_type=F32)


def _silu(x):
    return x * jax.nn.sigmoid(x)


def _ffn_body(layer_ref, x_ref, nw_ref, wg_ref, wu_ref, wd_ref, o_ref, h_ref, a_ref):
    del layer_ref
    x = x_ref[...]
    ms = jnp.mean(x * x, axis=-1, keepdims=True)
    h_ref[...] = (x * lax.rsqrt(ms + RMS_EPS) * nw_ref[...]).astype(BF16)
    for f in range(N_FF_TILES):
        cols = slice(f * FF_TILE, (f + 1) * FF_TILE)
        g = _dot(h_ref[...], wg_ref[:, cols])
        u = _dot(h_ref[...], wu_ref[:, cols])
        a_ref[:, cols] = (_silu(g) * u).astype(BF16)
    o_ref[...] = x_ref[...] + 0.5 * _dot(a_ref[...], wd_ref[...])


def _ffn(layer, x2d, norm_w, wg, wu, wd, *, tm):
    m = x2d.shape[0]
    grid_spec = pltpu.PrefetchScalarGridSpec(
        num_scalar_prefetch=1,
        grid=(m // tm,),
        in_specs=[
            pl.BlockSpec((tm, D_MODEL), lambda i, l: (i, 0)),
            pl.BlockSpec((None, 1, D_MODEL), lambda i, l: (l[0], 0, 0)),
            _resident((None, D_MODEL, D_FF), lambda i, l: (l[0], 0, 0)),
            _resident((None, D_MODEL, D_FF), lambda i, l: (l[0], 0, 0)),
            _resident((None, D_FF, D_MODEL), lambda i, l: (l[0], 0, 0)),
        ],
        out_specs=pl.BlockSpec((tm, D_MODEL), lambda i, l: (i, 0)),
        scratch_shapes=[pltpu.VMEM((tm, D_MODEL), BF16), pltpu.VMEM((tm, D_FF), BF16)],
    )
    return pl.pallas_call(
        _ffn_body, grid_spec=grid_spec,
        out_shape=jax.ShapeDtypeStruct((m, D_MODEL), F32),
        compiler_params=_params(1), name="ffn",
    )(layer, x2d, norm_w, wg, wu, wd)


def _softplus(x):
    return jnp.maximum(x, 0.0) + jnp.log1p(jnp.exp(-jnp.abs(x)))


def _mixin_body(layer_ref, x_ref, nw_ref, wm_ref, wba_ref, dnc_ref, hyc_ref, hyb_ref,
                alog_ref, dtb_ref, main_ref, st_ref, gct_ref, h_ref, cbuf_ref, *, seq, chunk):
    del layer_ref
    j = pl.program_id(1)
    n_chunks = seq // chunk
    pad = SUBLANES

    @pl.when(j == 0)
    def _():
        x = x_ref[...]
        ms = jnp.mean(x * x, axis=-1, keepdims=True)
        h_ref[...] = (x * lax.rsqrt(ms + RMS_EPS) * nw_ref[...]).astype(BF16)
        zeros = jnp.zeros((pad, COL_TILE), F32)
        cbuf_ref[0:pad, :] = zeros
        cbuf_ref[pad + seq:pad + seq + pad, :] = zeros

        ba = _dot(h_ref[...], wba_ref[...])
        lane = lax.broadcasted_iota(jnp.int32, (seq, LANES), 1)
        rowc = lax.broadcasted_iota(jnp.int32, (seq, LANES), 0) & (chunk - 1)
        n_bd = N_DIRS * DN_HEADS
        beta = jax.nn.sigmoid(ba)
        g = -jnp.exp(alog_ref[...]) * _softplus(ba + dtb_ref[...])
        g = jnp.where((lane >= n_bd) & (lane < 2 * n_bd), g, 0.0)
        pre = g
        suf = g
        s = 1
        while s < chunk:
            pre = pre + jnp.where(rowc >= s, pltpu.roll(pre, s, axis=0), 0.0)
            suf = suf + jnp.where(rowc < chunk - s, pltpu.roll(suf, seq - s, axis=0), 0.0)
            s *= 2
        is_fwd = lane < n_bd + DN_HEADS
        gc = jnp.where(is_fwd, pre, suf)
        pre3 = pre.reshape(n_chunks, chunk, LANES)
        suf3 = suf.reshape(n_chunks, chunk, LANES)
        tot_f = jnp.broadcast_to(pre3[:, chunk - 1:chunk, :], pre3.shape).reshape(seq, LANES)
        tot_b = jnp.broadcast_to(suf3[:, 0:1, :], suf3.shape).reshape(seq, LANES)
        gl = jnp.where(is_fwd, tot_f, tot_b)
        st0 = jnp.where(lane < n_bd, beta, gc)
        st_ref[0] = st0
        st_ref[1] = gl - gc
        st_ref[2] = gl
        gct = st0.T
        for c in range(n_chunks):
            gct_ref[c] = gct[n_bd:2 * n_bd, c * chunk:(c + 1) * chunk]

    cbuf_ref[pad:pad + seq, :] = _dot(h_ref[...], wm_ref[...])

    def conv_rows(r0, w_ref, k):
        win = cbuf_ref[pl.ds(r0, chunk + 2 * pad), :]
        acc = None
        for t in range(k):
            d = t - k // 2
            rolled = win if d == 0 else pltpu.roll(win, (-d) % (chunk + 2 * pad), axis=0)
            term = w_ref[t:t + 1, :] * rolled[pad:pad + chunk]
            acc = term if acc is None else acc + term
        return acc

    def l2norm_heads(t):
        outs = []
        for hh in range(COL_TILE // DN_DK):
            th = t[:, hh * DN_DK:(hh + 1) * DN_DK]
            outs.append(th * lax.rsqrt(jnp.sum(th * th, axis=-1, keepdims=True) + 1e-6))
        return jnp.concatenate(outs, axis=1)

    def rows(c):
        return pl.multiple_of(c * chunk, chunk)

    @pl.when(j == 0)
    def _():
        def body(c, carry):
            r0 = rows(c)
            t = l2norm_heads(_silu(conv_rows(r0, dnc_ref, DN_CONV)))
            main_ref[pl.ds(r0, chunk), :] = t * (DN_DK ** -0.5)
            return carry
        lax.fori_loop(0, n_chunks, body, 0)

    @pl.when(j == 1)
    def _():
        def body(c, carry):
            r0 = rows(c)
            main_ref[pl.ds(r0, chunk), :] = l2norm_heads(_silu(conv_rows(r0, dnc_ref, DN_CONV)))
            return carry
        lax.fori_loop(0, n_chunks, body, 0)

    @pl.when(j == 2)
    def _():
        def body(c, carry):
            r0 = rows(c)
            main_ref[pl.ds(r0, chunk), :] = _silu(conv_rows(r0, dnc_ref, DN_CONV))
            return carry
        lax.fori_loop(0, n_chunks, body, 0)

    @pl.when(j == 3)
    def _():
        def body(c, carry):
            r0 = rows(c)
            main_ref[pl.ds(r0, chunk), :] = _silu(cbuf_ref[pl.ds(r0 + pad, chunk), :])
            return carry
        lax.fori_loop(0, n_chunks, body, 0)

    @pl.when(j >= HY_COL0)
    def _():
        def body(c, carry):
            r0 = rows(c)
            main_ref[pl.ds(r0, chunk), :] = conv_rows(r0, hyc_ref, HY_SHORT) + hyb_ref[...]
            return carry
        lax.fori_loop(0, n_chunks, body, 0)


def _mixin(layer, x, norm_w, w_main, w_ba, dn_conv, hy_conv, hy_bias, a_log, dt_bias):
    b, seq, _ = x.shape
    chunk = DELTA_CHUNK
    n_chunks = seq // chunk
    n_bd = N_DIRS * DN_HEADS
    dn_tiles = (2 * DN_KEY + DN_VAL) // COL_TILE
    hy_tiles = 3 * HY_WIDTH // COL_TILE
    grid_spec = pltpu.PrefetchScalarGridSpec(
        num_scalar_prefetch=1,
        grid=(b, N_COL_TILES),
        in_specs=[
            pl.BlockSpec((None, seq, D_MODEL), lambda i, j, l: (i, 0, 0)),
            pl.BlockSpec((None, 1, D_MODEL), lambda i, j, l: (l[0], 0, 0)),
            pl.BlockSpec((None, D_MODEL, COL_TILE), lambda i, j, l: (l[0], 0, j)),
            pl.BlockSpec((None, D_MODEL, LANES), lambda i, j, l: (l[0], 0, 0)),
            pl.BlockSpec((None, DN_CONV, COL_TILE),
                         lambda i, j, l: (l[0], 0, jnp.minimum(j, dn_tiles - 1))),
            pl.BlockSpec((None, HY_SHORT, COL_TILE),
                         lambda i, j, l: (l[0], 0, jnp.clip(j - HY_COL0, 0, hy_tiles - 1))),
            pl.BlockSpec((None, 1, COL_TILE),
                         lambda i, j, l: (l[0], 0, jnp.clip(j - HY_COL0, 0, hy_tiles - 1))),
            pl.BlockSpec((None, 1, LANES), lambda i, j, l: (l[0], 0, 0)),
            pl.BlockSpec((None, 1, LANES), lambda i, j, l: (l[0], 0, 0)),
        ],
        out_specs=[
            pl.BlockSpec((None, seq, COL_TILE), lambda i, j, l: (i, 0, j)),
            pl.BlockSpec((None, 3, seq, LANES), lambda i, j, l: (i, 0, 0, 0)),
            pl.BlockSpec((None, n_chunks, n_bd, chunk), lambda i, j, l: (i, 0, 0, 0)),
        ],
        scratch_shapes=[pltpu.VMEM((seq, D_MODEL), BF16),
                        pltpu.VMEM((seq + 2 * SUBLANES, COL_TILE), F32)],
    )
    return pl.pallas_call(
        functools.partial(_mixin_body, seq=seq, chunk=chunk), grid_spec=grid_spec,
        out_shape=[jax.ShapeDtypeStruct((b, seq, N_MAIN), F32),
                   jax.ShapeDtypeStruct((b, 3, seq, LANES), F32),
                   jax.ShapeDtypeStruct((b, n_chunks, n_bd, chunk), F32)],
        compiler_params=_params(2), name="mixin",
    )(layer, x, norm_w, w_main, w_ba, dn_conv, hy_conv, hy_bias, a_log, dt_bias)


_INV_LEVELS = (8, 16, 32, 64, 128)


def _delta_body(layer_ref, q_ref, k_ref, v_ref, z_ref, st_ref, gct_ref, nw_ref, o_ref,
                wq_s, u_s, att_s, kdt_s, gam_s, s_s, oacc_s, *, seq):
    del layer_ref
    hp = pl.program_id(1)
    heads = range(HEADS_PER_STEP)
    c_ = DELTA_CHUNK
    n_chunks = seq // c_
    n_bd = N_DIRS * DN_HEADS
    row = lax.broadcasted_iota(jnp.int32, (c_, c_), 0)
    col = lax.broadcasted_iota(jnp.int32, (c_, c_), 1)
    lane = lax.broadcasted_iota(jnp.int32, (c_, LANES), 1)

    half_row = lax.broadcasted_iota(jnp.int32, (c_ // 2, c_), 0)
    half_col = lax.broadcasted_iota(jnp.int32, (c_ // 2, c_), 1)

    def pick(x, idx):
        return jnp.sum(jnp.where(lane == idx, x, 0.0), axis=1, keepdims=True)

    par = math.gcd(max(PAR_CHUNKS // HEADS_PER_STEP, 1), n_chunks)

    def phase_a(it, carry):
        chains = []
        for hh, j in itertools.product(heads, range(par)):
            hd = hp * HEADS_PER_STEP + hh
            head_lanes = slice(hh * DN_DK, (hh + 1) * DN_DK)
            g = it * par + j
            r0 = pl.multiple_of(g * c_, c_)
            q = q_ref[pl.ds(r0, c_), head_lanes]
            k = k_ref[pl.ds(r0, c_), head_lanes]
            v = v_ref[pl.ds(r0, c_), head_lanes]
            st0 = st_ref[0, pl.ds(r0, c_), :]
            st1 = st_ref[1, pl.ds(r0, c_), :]
            st2 = st_ref[2, pl.ds(r0, c_), :]
            k_bf = k.astype(BF16)
            q_bf = q.astype(BF16)
            for d in range(N_DIRS):
                ch = types.SimpleNamespace(d=d, step=g if d == 0 else n_chunks - 1 - g, hh=hh,
                                           lanes=slice(d * DN_DV, (d + 1) * DN_DV), k_bf=k_bf)
                bi = d * DN_HEADS + hd
                beta = pick(st0, bi)
                gc = pick(st0, n_bd + bi)
                egc = jnp.exp(gc)
                ekd = jnp.exp(pick(st1, n_bd + bi))
                egl = jnp.exp(pick(st2, n_bd + bi))
                gcr = gct_ref[g, pl.ds(bi, 1), :]
                incl = (row >= col) if d == 0 else (row <= col)
                ch.strict = (row > col) if d == 0 else (row < col)
                ch.dec = jnp.where(incl, jnp.exp(jnp.where(incl, gc - gcr, 0.0)), 0.0)
                kb = k * beta
                ch.lhs = jnp.concatenate([kb.astype(BF16), q_bf], axis=0)
                ch.rhs = jnp.concatenate([(v * beta).astype(BF16), (kb * egc).astype(BF16)], axis=1)
                wq_s[hh, ch.step, c_:2 * c_, ch.lanes] = (q * egc).astype(BF16)
                kdt_s[hh, ch.step, d * DN_DK:(d + 1) * DN_DK, :] = (k * ekd).T.astype(BF16)
                gam_s[hh, ch.step, d * DN_DK:(d + 1) * DN_DK, :] = jnp.broadcast_to(
                    egl[0:DN_DK, :], (DN_DK, N_DIRS * DN_DV))
                chains.append(ch)

        for ch in chains:
            ch.s1 = lax.dot_general(ch.lhs, ch.k_bf, (((1,), (1,)), ((), ())),
                                    preferred_element_type=F32)
        for ch in chains:
            a = jnp.where(ch.strict, ch.s1[:c_] * ch.dec, 0.0)
            att_s[ch.hh, ch.step, :, ch.d * c_:(ch.d + 1) * c_] = (ch.s1[c_:] * ch.dec).astype(BF16)
            ch.a = a
            ch.b0 = jnp.where((row >> 3) == (col >> 3), -a, 0.0)
            ch.b0_bf = ch.b0.astype(BF16)
        for ch in chains:
            ch.r1 = _dot(ch.b0_bf, ch.b0_bf)
        for ch in chains:
            r1_bf = ch.r1.astype(BF16)
            ch.m = _dot(jnp.concatenate([ch.b0_bf, r1_bf], axis=0), r1_bf)
        for ch in chains:
            ch.q2 = ch.b0 + ch.r1 + ch.m[:c_]
            ch.r2 = ch.m[c_:]
            ch.q2r2 = _dot(ch.q2.astype(BF16), ch.r2.astype(BF16))
        for ch in chains:
            ch.xf = jnp.where(row == col, 1.0, ch.q2 + ch.r2 + ch.q2r2)
            ch.x = ch.xf.astype(BF16)
        for s in _INV_LEVELS:
            sh = s.bit_length() - 1
            n_pairs = c_ // (2 * s)

            def take(m, odd, s=s, n_pairs=n_pairs):
                return jnp.concatenate(
                    [m[(2 * i + odd) * s:(2 * i + odd + 1) * s] for i in range(n_pairs)], axis=0)

            def interleave(even, odd, s=s, n_pairs=n_pairs):
                pieces = []
                for i in range(n_pairs):
                    pieces += [even[i * s:(i + 1) * s], odd[i * s:(i + 1) * s]]
                return jnp.concatenate(pieces, axis=0)

            for ch in chains:
                act = 1 - ch.d
                partner_cols = (half_col >> sh) == (((half_row >> sh) << 1) + ch.d)
                e = jnp.where(partner_cols, take(ch.a, act), 0.0)
                ch.m1 = _dot(e.astype(BF16), ch.x)
            for ch in chains:
                act = 1 - ch.d
                zeros = jnp.zeros_like(ch.m1)
                m1_rows = interleave(zeros, -ch.m1) if act else interleave(-ch.m1, zeros)
                ch.m2 = _dot(take(ch.xf, act).astype(BF16), m1_rows.astype(BF16))
            for ch in chains:
                act = 1 - ch.d
                new = take(ch.xf, act) + ch.m2
                keep = take(ch.xf, ch.d)
                ch.xf = interleave(keep, new) if act else interleave(new, keep)
                ch.x = ch.xf.astype(BF16)
        for ch in chains:
            ch.uw = _dot(ch.x, ch.rhs)
        for ch in chains:
            u_s[ch.hh, ch.step, :, ch.lanes] = ch.uw[:, :DN_DV]
            wq_s[ch.hh, ch.step, 0:c_, ch.lanes] = ch.uw[:, DN_DV:].astype(BF16)
        return carry

    lax.fori_loop(0, n_chunks // par, phase_a, 0)

    s_s[...] = jnp.zeros_like(s_s)
    oacc_s[...] = jnp.zeros_like(oacc_s)
    w2 = N_DIRS * DN_DV
    lane2 = lax.broadcasted_iota(jnp.int32, (c_, w2), 1)
    blockdiag = (lax.broadcasted_iota(jnp.int32, (w2, w2), 0) >> 7) == (
        lax.broadcasted_iota(jnp.int32, (w2, w2), 1) >> 7)

    def scan_step(s, carry):
        rf = pl.multiple_of(s * c_, c_)
        rb = pl.multiple_of((n_chunks - 1 - s) * c_, c_)
        wqs = [_dot(wq_s[hh, s], s_s[hh].astype(BF16)) for hh in heads]
        vn = [u_s[hh, s] - wqs[hh][:c_] for hh in heads]
        kv = [_dot(kdt_s[hh, s], vn[hh].astype(BF16)) for hh in heads]
        for hh in heads:
            s_s[hh] = s_s[hh] * gam_s[hh, s] + jnp.where(blockdiag, kv[hh], 0.0)
        for hh in heads:
            rhs = jnp.concatenate([jnp.where(lane2 < DN_DV, vn[hh], 0.0).astype(BF16),
                                   jnp.where(lane2 >= DN_DV, vn[hh], 0.0).astype(BF16)], axis=0)
            o = wqs[hh][c_:] + _dot(att_s[hh, s], rhs)
            oacc_s[hh, pl.ds(rf, c_), :] += o[:, :DN_DV]
            oacc_s[hh, pl.ds(rb, c_), :] += o[:, DN_DV:]
        return carry

    lax.fori_loop(0, n_chunks, scan_step, 0)

    for hh in heads:
        head_lanes = slice(hh * DN_DV, (hh + 1) * DN_DV)
        o = oacc_s[hh]
        ms = jnp.mean(o * o, axis=-1, keepdims=True)
        o_ref[:, head_lanes] = o * lax.rsqrt(ms + RMS_EPS) * nw_ref[...] * z_ref[:, head_lanes]


def _delta(layer, main, st, gct, norm_w):
    b, seq, _ = main.shape
    c_ = DELTA_CHUNK
    n_chunks = seq // c_
    n_bd = N_DIRS * DN_HEADS
    w2 = N_DIRS * DN_DV
    hps = HEADS_PER_STEP
    groups = DN_HEADS // hps
    head_spec = lambda part: pl.BlockSpec(
        (None, seq, hps * DN_DK), lambda i, h, l: (i, 0, part * groups + h))
    grid_spec = pltpu.PrefetchScalarGridSpec(
        num_scalar_prefetch=1,
        grid=(b, groups),
        in_specs=[
            head_spec(0), head_spec(1), head_spec(2), head_spec(3),
            pl.BlockSpec((None, 3, seq, LANES), lambda i, h, l: (i, 0, 0, 0)),
            pl.BlockSpec((None, n_chunks, n_bd, c_), lambda i, h, l: (i, 0, 0, 0)),
            pl.BlockSpec((None, 1, DN_DV), lambda i, h, l: (l[0], 0, 0)),
        ],
        out_specs=pl.BlockSpec((None, seq, hps * DN_DV), lambda i, h, l: (i, 0, h)),
        scratch_shapes=[
            pltpu.VMEM((hps, n_chunks, 2 * c_, w2), BF16),
            pltpu.VMEM((hps, n_chunks, c_, w2), F32),
            pltpu.VMEM((hps, n_chunks, c_, N_DIRS * c_), BF16),
            pltpu.VMEM((hps, n_chunks, N_DIRS * DN_DK, c_), BF16),
            pltpu.VMEM((hps, n_chunks, N_DIRS * DN_DK, w2), F32),
            pltpu.VMEM((hps, N_DIRS * DN_DK, w2), F32),
            pltpu.VMEM((hps, seq, DN_DV), F32),
        ],
    )
    return pl.pallas_call(
        functools.partial(_delta_body, seq=seq), grid_spec=grid_spec,
        out_shape=jax.ShapeDtypeStruct((b, seq, DN_VAL), F32),
        compiler_params=_params(2), name="delta",
    )(layer, main, main, main, main, st, gct, norm_w)


def _dft_tables(seq):
    n = 2 * seq
    k = jnp.arange(seq, dtype=jnp.int32)[:, None]

    def trig(t):
        ang = ((k * t[None, :]) & (n - 1)).astype(F32) * (2.0 * math.pi / n)
        return jnp.cos(ang), jnp.sin(ang)

    c_hi, s_hi = trig(jnp.arange(seq // LANES, dtype=jnp.int32) * LANES)
    c_lo, s_lo = trig(jnp.arange(LANES, dtype=jnp.int32))
    c_hi, s_hi = c_hi[:, :, None], s_hi[:, :, None]
    c_lo, s_lo = c_lo[:, None, :], s_lo[:, None, :]
    cos = (c_hi * c_lo - s_hi * s_lo).reshape(seq, seq)
    sin = (s_hi * c_lo + c_hi * s_lo).reshape(seq, seq)
    return cos.astype(BF16), sin.astype(BF16)


def _filter_feats(seq):
    t = jnp.linspace(0.0, 1.0, seq, dtype=F32)[:, None]
    bands = (HY_EMB - 1) // 2
    ang = ((2.0 * math.pi / seq) * jnp.arange(seq, dtype=F32)[:, None]
           * jnp.linspace(1e-4, bands - 1, bands, dtype=F32)[None, :])
    feats = jnp.concatenate([t, jnp.cos(ang), -jnp.sin(ang)], axis=-1)
    return jnp.pad(feats, ((0, 0), (0, PAD_HID - HY_EMB)))


def _filter_deltas():
    max_decay = math.log(HY_DECAY_TARGET) / HY_FAST_DECAY_PCT
    min_decay = math.log(HY_DECAY_TARGET) / HY_SLOW_DECAY_PCT
    return jnp.abs(jnp.linspace(min_decay, max_decay, HY_WIDTH, dtype=F32))[None, :]


def _filt_body(layer_ref, feats_ref, w1_ref, b1_ref, w2_ref, b2_ref, w3_ref, b3_ref, fr_ref,
               wof_ref, wob_ref, dl_ref, c_ref, s_ref, kre_ref, kim_ref, hdn_ref, *, seq):
    del layer_ref
    n = 2 * seq

    @pl.when(pl.program_id(0) == 0)
    def _():
        fr = fr_ref[...]
        hp = functools.partial(jnp.dot, preferred_element_type=F32, precision=HIGHEST)
        h = jnp.sin(fr * (hp(feats_ref[...], w1_ref[...]) + b1_ref[...]))
        h = jnp.sin(fr * (hp(h, w2_ref[...]) + b2_ref[...]))
        hdn_ref[...] = jnp.sin(fr * (hp(h, w3_ref[...]) + b3_ref[...]))

    rowi = lax.broadcasted_iota(jnp.int32, (seq, HY_CT), 0)
    tt = rowi.astype(F32) * (1.0 / (seq - 1))
    window = jnp.exp(-tt * dl_ref[...])
    def split(a):
        hi = a.astype(BF16)
        return hi, (a - hi.astype(F32)).astype(BF16)

    h_hi, h_lo = split(hdn_ref[...])

    def filt(w_ref):
        w_hi, w_lo = split(w_ref[...])
        return _dot(h_hi, w_hi) + (_dot(h_hi, w_lo) + _dot(h_lo, w_hi))

    fwd = filt(wof_ref) * window
    bwd = filt(wob_ref) * window
    bsh = jnp.where(rowi >= 1, pltpu.roll(bwd, 1, axis=0), 0.0)
    rhs = jnp.concatenate([fwd, bsh], axis=1).astype(BF16)
    gc = _dot(c_ref[...], rhs)
    gs = _dot(s_ref[...], rhs)
    alt = (1 - 2 * (rowi & 1)).astype(F32)
    nyq = jnp.sum(alt * (fwd + bsh), axis=0, keepdims=True)
    kre = gc[:, :HY_CT] + gc[:, HY_CT:]
    kim = gs[:, HY_CT:] - gs[:, :HY_CT]
    first = rowi == 0
    kre_ref[...] = kre * jnp.where(first, 1.0 / n, 2.0 / n)
    kim_ref[...] = jnp.where(first, nyq * (1.0 / n), kim * (2.0 / n))


def _filter_spectrum(layer, feats, w1, b1, w2, b2, w3, b3, freq, wout, deltas, ctab, stab):
    seq = feats.shape[0]
    n_oc = HY_ORDER * HY_WIDTH
    tiles = n_oc // HY_CT
    per_w = HY_WIDTH // HY_CT
    sq = lambda: pl.BlockSpec((None, PAD_HID, PAD_HID), lambda j, l: (l[0], 0, 0))
    vec = lambda: pl.BlockSpec((None, 1, PAD_HID), lambda j, l: (l[0], 0, 0))
    grid_spec = pltpu.PrefetchScalarGridSpec(
        num_scalar_prefetch=1,
        grid=(tiles,),
        in_specs=[
            pl.BlockSpec((seq, PAD_HID), lambda j, l: (0, 0)),
            sq(), vec(), sq(), vec(), sq(), vec(), vec(),
            pl.BlockSpec((None, PAD_HID, HY_CT), lambda j, l: (l[0], 0, j)),
            pl.BlockSpec((None, PAD_HID, HY_CT), lambda j, l: (l[0], 0, tiles + j)),
            pl.BlockSpec((1, HY_CT), lambda j, l: (0, j % per_w)),
            _resident((seq, seq), lambda j, l: (0, 0)),
            _resident((seq, seq), lambda j, l: (0, 0)),
        ],
        out_specs=[pl.BlockSpec((None, seq, HY_CT), lambda j, l: (j % per_w, 0, j // per_w)),
                   pl.BlockSpec((None, seq, HY_CT), lambda j, l: (j % per_w, 0, j // per_w))],
        scratch_shapes=[pltpu.VMEM((seq, PAD_HID), F32)],
    )
    return pl.pallas_call(
        functools.partial(_filt_body, seq=seq), grid_spec=grid_spec,
        out_shape=[jax.ShapeDtypeStruct((per_w, seq, HY_ORDER * HY_CT), F32)] * 2,
        compiler_params=_params(1), name="hyena_filter",
    )(layer, feats, w1, b1, w2, b2, w3, b3, freq, wout, wout, deltas, ctab, stab)


def _hyena_body(layer_ref, v_ref, x1_ref, x2_ref, kre_ref, kim_ref, skip_ref, c_ref, s_ref, o_ref,
                z_s, xc_s, xs_s, ya_s, yb_s, *, seq):
    del layer_ref
    rc = HY_ROWS
    n_rc = seq // rc
    alt = (1 - 2 * (lax.broadcasted_iota(jnp.int32, (rc, HY_CT), 0) & 1)).astype(F32)

    def rows(c):
        return pl.ds(pl.multiple_of(c * rc, rc), rc)

    for o, gate_ref in enumerate((x1_ref, x2_ref)):
        z_ref = v_ref if o == 0 else z_s
        last = o == HY_ORDER - 1
        cols = slice(o * HY_CT, (o + 1) * HY_CT)

        def to_bf16(c, acc):
            z = z_ref[rows(c), :]
            ya_s[rows(c), :] = z.astype(BF16)
            yb_s[rows(c), :] = z.astype(BF16)
            return acc + jnp.sum(alt * z, axis=0, keepdims=True)

        xnyq = lax.fori_loop(0, n_rc, to_bf16, jnp.zeros((1, HY_CT), F32))
        xc_s[...] = _dot(c_ref[...], ya_s[...])
        xs_s[...] = _dot(s_ref[...], yb_s[...])

        def spectrum(c, carry):
            xc, xs = xc_s[rows(c), :], xs_s[rows(c), :]
            kre, kim = kre_ref[rows(c), cols], kim_ref[rows(c), cols]
            ya_s[rows(c), :] = (xc * kre + xs * kim).astype(BF16)
            yb_s[rows(c), :] = (xs * kre - xc * kim).astype(BF16)
            return carry

        lax.fori_loop(0, n_rc, spectrum, 0)
        xc_s[...] = _dot(c_ref[...], ya_s[...])
        xs_s[...] = _dot(s_ref[...], yb_s[...])
        ynyq = xnyq * kim_ref[0:1, cols]
        skip = skip_ref[o:o + 1, :]

        def gate(c, carry):
            z = z_ref[rows(c), :]
            y = xc_s[rows(c), :] + xs_s[rows(c), :] + alt * ynyq
            z_new = gate_ref[rows(c), :] * (y + skip * z)
            (o_ref if last else z_s)[rows(c), :] = z_new
            return carry

        lax.fori_loop(0, n_rc, gate, 0)


def _hyena(layer, main, kre, kim, skip, ctab, stab):
    b, seq, _ = main.shape
    tiles = HY_WIDTH // HY_CT
    col0 = (2 * DN_KEY + 2 * DN_VAL) // HY_CT
    hy_spec = lambda part: pl.BlockSpec(
        (None, seq, HY_CT), lambda j, i, l: (i, 0, col0 + part * tiles + j))
    k_spec = lambda: pl.BlockSpec((None, seq, HY_ORDER * HY_CT), lambda j, i, l: (j, 0, 0),
                                  pipeline_mode=pl.Buffered(1))
    grid_spec = pltpu.PrefetchScalarGridSpec(
        num_scalar_prefetch=1,
        grid=(tiles, b),
        in_specs=[
            hy_spec(0), hy_spec(1), hy_spec(2), k_spec(), k_spec(),
            pl.BlockSpec((None, None, HY_ORDER, HY_CT), lambda j, i, l: (l[0], j, 0, 0)),
            _resident((seq, seq), lambda j, i, l: (0, 0)),
            _resident((seq, seq), lambda j, i, l: (0, 0)),
        ],
        out_specs=pl.BlockSpec((None, seq, HY_CT), lambda j, i, l: (i, 0, j)),
        scratch_shapes=[pltpu.VMEM((seq, HY_CT), F32),
                        pltpu.VMEM((seq, HY_CT), F32), pltpu.VMEM((seq, HY_CT), F32),
                        pltpu.VMEM((seq, HY_CT), BF16), pltpu.VMEM((seq, HY_CT), BF16)],
    )
    return pl.pallas_call(
        functools.partial(_hyena_body, seq=seq), grid_spec=grid_spec,
        out_shape=jax.ShapeDtypeStruct((b, seq, HY_WIDTH), F32),
        compiler_params=_params(2), name="hyena",
    )(layer, main, main, main, kre, kim, skip, ctab, stab)


def _mixout_body(layer_ref, x_ref, odn_ref, zz_ref, hn_ref, wo_ref, o_ref):
    del layer_ref
    zz = zz_ref[...]
    ms = jnp.mean(zz * zz, axis=-1, keepdims=True)
    zn = zz * lax.rsqrt(ms + RMS_EPS) * hn_ref[...]
    y = _dot(odn_ref[...].astype(BF16), wo_ref[0:DN_VAL, :])
    y = y + _dot(zn.astype(BF16), wo_ref[DN_VAL:D_MODEL, :])
    o_ref[...] = x_ref[...] + y


def _mixout(layer, x2d, odn2d, zz2d, hy_norm, w_out, *, tm):
    m = x2d.shape[0]
    grid_spec = pltpu.PrefetchScalarGridSpec(
        num_scalar_prefetch=1,
        grid=(m // tm,),
        in_specs=[
            pl.BlockSpec((tm, D_MODEL), lambda i, l: (i, 0)),
            pl.BlockSpec((tm, DN_VAL), lambda i, l: (i, 0)),
            pl.BlockSpec((tm, HY_WIDTH), lambda i, l: (i, 0)),
            pl.BlockSpec((None, 1, HY_WIDTH), lambda i, l: (l[0], 0, 0)),
            pl.BlockSpec((None, D_MODEL, D_MODEL), lambda i, l: (l[0], 0, 0)),
        ],
        out_specs=pl.BlockSpec((tm, D_MODEL), lambda i, l: (i, 0)),
    )
    return pl.pallas_call(
        _mixout_body, grid_spec=grid_spec,
        out_shape=jax.ShapeDtypeStruct((m, D_MODEL), F32),
        compiler_params=_params(1), name="mixout",
    )(layer, x2d, odn2d, zz2d, hy_norm, w_out)


def _final_norm_body(x_ref, w_ref, o_ref):
    x = x_ref[...]
    ms = jnp.mean(x * x, axis=-1, keepdims=True)
    o_ref[...] = x * lax.rsqrt(ms + RMS_EPS) * w_ref[...]


def _final_norm(x2d, w, *, tm):
    m = x2d.shape[0]
    return pl.pallas_call(
        _final_norm_body, grid=(m // tm,),
        in_specs=[pl.BlockSpec((tm, D_MODEL), lambda i: (i, 0)),
                  pl.BlockSpec((1, D_MODEL), lambda i: (0, 0))],
        out_specs=pl.BlockSpec((tm, D_MODEL), lambda i: (i, 0)),
        out_shape=jax.ShapeDtypeStruct((m, D_MODEL), F32),
        compiler_params=_params(1), name="final_norm",
    )(x2d, w)


def _pad_to(a, axis, size):
    pads = [(0, 0)] * a.ndim
    pads[axis] = (0, size - a.shape[axis])
    return jnp.pad(a, pads)


def _row_tile(m):
    return 1024 if m % 1024 == 0 else m


def kernel(x, ffn1_norm, ffn1_w_gate, ffn1_w_up, ffn1_w_down, mix_norm, w_in, dn_conv, dn_a_log, dn_dt_bias, dn_norm, hy_conv, hy_conv_bias, hy_f_w1, hy_f_b1, hy_f_w2, hy_f_b2, hy_f_w3, hy_f_b3, hy_f_freq, hy_f_wout, hy_skip, hy_norm, w_out, ffn2_norm, ffn2_w_gate, ffn2_w_up, ffn2_w_down, final_norm):
    b, seq, _ = x.shape
    m = b * seq
    tm = _row_tile(m)
    n_bd = N_DIRS * DN_HEADS
    bf = lambda a: a.astype(BF16)
    row3 = lambda a: a[:, None, :]

    ba0 = 2 * DN_KEY + 2 * DN_VAL
    w_in_bf = bf(w_in)
    w_main = jnp.concatenate([w_in_bf[:, :, :ba0], w_in_bf[:, :, ba0 + 2 * n_bd:]], axis=-1)
    w_ba = _pad_to(w_in_bf[:, :, ba0:ba0 + 2 * n_bd], 2, LANES)
    lane_params = lambda p: _pad_to(
        jnp.pad(p.reshape(DEPTH, 1, n_bd), ((0, 0), (0, 0), (n_bd, 0))), 2, LANES)
    a_log = lane_params(dn_a_log)
    dt_bias = lane_params(dn_dt_bias)
    f_w1 = _pad_to(_pad_to(hy_f_w1, 1, PAD_HID), 2, PAD_HID)
    f_w2 = _pad_to(_pad_to(hy_f_w2, 1, PAD_HID), 2, PAD_HID)
    f_w3 = _pad_to(_pad_to(hy_f_w3, 1, PAD_HID), 2, PAD_HID)
    f_b1, f_b2, f_b3, f_fr = (row3(_pad_to(p, 1, PAD_HID))
                              for p in (hy_f_b1, hy_f_b2, hy_f_b3, hy_f_freq))
    f_wout = _pad_to(hy_f_wout, 1, PAD_HID)
    tiles = HY_WIDTH // HY_CT
    skip = hy_skip.reshape(DEPTH, HY_ORDER, tiles, HY_CT).transpose(0, 2, 1, 3)
    weights = dict(
        ffn1=(row3(ffn1_norm), bf(ffn1_w_gate), bf(ffn1_w_up), bf(ffn1_w_down)),
        ffn2=(row3(ffn2_norm), bf(ffn2_w_gate), bf(ffn2_w_up), bf(ffn2_w_down)),
        w_out=bf(w_out))

    ctab, stab = lax.optimization_barrier(_dft_tables(seq))
    feats = _filter_feats(seq)
    deltas = _filter_deltas()

    def layer_fn(i, carry):
        xc, ctab, stab = carry
        layer = jnp.full((1,), i, jnp.int32)
        x2 = _ffn(layer, xc.reshape(m, D_MODEL), *weights["ffn1"], tm=tm)
        x3 = x2.reshape(b, seq, D_MODEL)
        main, st, gct = _mixin(layer, x3, row3(mix_norm), w_main, w_ba, dn_conv, hy_conv,
                               row3(hy_conv_bias), a_log, dt_bias)
        kre, kim = _filter_spectrum(layer, feats, f_w1, f_b1, f_w2, f_b2, f_w3, f_b3, f_fr,
                                    f_wout, deltas, ctab, stab)
        o_dn = _delta(layer, main, st, gct, row3(dn_norm))
        zz = _hyena(layer, main, kre, kim, skip, ctab, stab)
        x4 = _mixout(layer, x2, o_dn.reshape(m, DN_VAL), zz.reshape(m, HY_WIDTH),
                     row3(hy_norm), weights["w_out"], tm=tm)
        x5 = _ffn(layer, x4, *weights["ffn2"], tm=tm)
        return x5.reshape(b, seq, D_MODEL), ctab, stab

    xf, _, _ = lax.fori_loop(0, DEPTH, layer_fn, (x, ctab, stab))
    return _final_norm(xf.reshape(m, D_MODEL), final_norm[None, :], tm=tm).reshape(b, seq, D_MODEL)
```

```python
import functools
import itertools
import math
import types

import jax
import jax.numpy as jnp
from jax import lax
from jax.experimental import pallas as pl
from jax.experimental.pallas import tpu as pltpu

F32 = jnp.float32
BF16 = jnp.bfloat16
HIGHEST = lax.Precision.HIGHEST

D_MODEL = 1024
DEPTH = 4
N_DIRS = 2
DN_HEADS = 4
DN_DK = 128
DN_DV = 128
DN_KEY = DN_HEADS * DN_DK
DN_VAL = DN_HEADS * DN_DV
DN_CONV = 5
HY_WIDTH = D_MODEL - DN_VAL
HY_ORDER = 2
HY_SHORT = 3
HY_EMB = 33
HY_HID = 64
HY_DIRS = 2
HY_FAST_DECAY_PCT = 0.3
HY_SLOW_DECAY_PCT = 1.5
HY_DECAY_TARGET = 1e-2
D_FF = 2816
RMS_EPS = 1e-6

LANES = 128
SUBLANES = 8
MXU_DIM = 256
VMEM_LIMIT_BYTES = 58 * 1024 * 1024

N_MAIN = 2 * DN_KEY + 2 * DN_VAL + 3 * HY_WIDTH
COL_TILE = 512
N_COL_TILES = N_MAIN // COL_TILE
HY_COL0 = (2 * DN_KEY + 2 * DN_VAL) // COL_TILE
FF_TILE = MXU_DIM
N_FF_TILES = D_FF // FF_TILE
DELTA_CHUNK = MXU_DIM
HEADS_PER_STEP = 2
PAR_CHUNKS = 4
HY_CT = MXU_DIM
HY_ROWS = 256
PAD_HID = LANES


def _params(n_grid):
    return pltpu.CompilerParams(
        dimension_semantics=("arbitrary",) * n_grid,
        vmem_limit_bytes=VMEM_LIMIT_BYTES)


def _resident(block, imap):
    return pl.BlockSpec(block, imap, pipeline_mode=pl.Buffered(1))


def _dot(a, b):
    return jnp.dot(a, b, preferred_element_type=F32)


def _silu(x):
    return x * jax.nn.sigmoid(x)


def _ffn_body(layer_ref, x_ref, nw_ref, wg_ref, wu_ref, wd_ref, o_ref, h_ref, a_ref):
    del layer_ref
    x = x_ref[...]
    ms = jnp.mean(x * x, axis=-1, keepdims=True)
    h_ref[...] = (x * lax.rsqrt(ms + RMS_EPS) * nw_ref[...]).astype(BF16)
    for f in range(N_FF_TILES):
        cols = slice(f * FF_TILE, (f + 1) * FF_TILE)
        g = _dot(h_ref[...], wg_ref[:, cols])
        u = _dot(h_ref[...], wu_ref[:, cols])
        a_ref[:, cols] = (_silu(g) * u).astype(BF16)
    o_ref[...] = x_ref[...] + 0.5 * _dot(a_ref[...], wd_ref[...])


def _ffn(layer, x2d, norm_w, wg, wu, wd, *, tm):
    m = x2d.shape[0]
    grid_spec = pltpu.PrefetchScalarGridSpec(
        num_scalar_prefetch=1,
        grid=(m // tm,),
        in_specs=[
            pl.BlockSpec((tm, D_MODEL), lambda i, l: (i, 0)),
            pl.BlockSpec((None, 1, D_MODEL), lambda i, l: (l[0], 0, 0)),
            _resident((None, D_MODEL, D_FF), lambda i, l: (l[0], 0, 0)),
            _resident((None, D_MODEL, D_FF), lambda i, l: (l[0], 0, 0)),
            _resident((None, D_FF, D_MODEL), lambda i, l: (l[0], 0, 0)),
        ],
        out_specs=pl.BlockSpec((tm, D_MODEL), lambda i, l: (i, 0)),
        scratch_shapes=[pltpu.VMEM((tm, D_MODEL), BF16), pltpu.VMEM((tm, D_FF), BF16)],
    )
    return pl.pallas_call(
        _ffn_body, grid_spec=grid_spec,
        out_shape=jax.ShapeDtypeStruct((m, D_MODEL), F32),
        compiler_params=_params(1), name="ffn",
    )(layer, x2d, norm_w, wg, wu, wd)


def _softplus(x):
    return jnp.maximum(x, 0.0) + jnp.log1p(jnp.exp(-jnp.abs(x)))


def _mixin_body(layer_ref, x_ref, nw_ref, wm_ref, wba_ref, dnc_ref, hyc_ref, hyb_ref,
                alog_ref, dtb_ref, main_ref, st_ref, gct_ref, h_ref, cbuf_ref, *, seq, chunk):
    del layer_ref
    j = pl.program_id(1)
    n_chunks = seq // chunk
    pad = SUBLANES

    @pl.when(j == 0)
    def _():
        x = x_ref[...]
        ms = jnp.mean(x * x, axis=-1, keepdims=True)
        h_ref[...] = (x * lax.rsqrt(ms + RMS_EPS) * nw_ref[...]).astype(BF16)
        zeros = jnp.zeros((pad, COL_TILE), F32)
        cbuf_ref[0:pad, :] = zeros
        cbuf_ref[pad + seq:pad + seq + pad, :] = zeros

        ba = _dot(h_ref[...], wba_ref[...])
        lane = lax.broadcasted_iota(jnp.int32, (seq, LANES), 1)
        rowc = lax.broadcasted_iota(jnp.int32, (seq, LANES), 0) & (chunk - 1)
        n_bd = N_DIRS * DN_HEADS
        beta = jax.nn.sigmoid(ba)
        g = -jnp.exp(alog_ref[...]) * _softplus(ba + dtb_ref[...])
        g = jnp.where((lane >= n_bd) & (lane < 2 * n_bd), g, 0.0)
        pre = g
        suf = g
        s = 1
        while s < chunk:
            pre = pre + jnp.where(rowc >= s, pltpu.roll(pre, s, axis=0), 0.0)
            suf = suf + jnp.where(rowc < chunk - s, pltpu.roll(suf, seq - s, axis=0), 0.0)
            s *= 2
        is_fwd = lane < n_bd + DN_HEADS
        gc = jnp.where(is_fwd, pre, suf)
        pre3 = pre.reshape(n_chunks, chunk, LANES)
        suf3 = suf.reshape(n_chunks, chunk, LANES)
        tot_f = jnp.broadcast_to(pre3[:, chunk - 1:chunk, :], pre3.shape).reshape(seq, LANES)
        tot_b = jnp.broadcast_to(suf3[:, 0:1, :], suf3.shape).reshape(seq, LANES)
        gl = jnp.where(is_fwd, tot_f, tot_b)
        st0 = jnp.where(lane < n_bd, beta, gc)
        st_ref[0] = st0
        st_ref[1] = gl - gc
        st_ref[2] = gl
        gct = st0.T
        for c in range(n_chunks):
            gct_ref[c] = gct[n_bd:2 * n_bd, c * chunk:(c + 1) * chunk]

    cbuf_ref[pad:pad + seq, :] = _dot(h_ref[...], wm_ref[...])

    def conv_rows(r0, w_ref, k):
        win = cbuf_ref[pl.ds(r0, chunk + 2 * pad), :]
        acc = None
        for t in range(k):
            d = t - k // 2
            rolled = win if d == 0 else pltpu.roll(win, (-d) % (chunk + 2 * pad), axis=0)
            term = w_ref[t:t + 1, :] * rolled[pad:pad + chunk]
            acc = term if acc is None else acc + term
        return acc

    def l2norm_heads(t):
        outs = []
        for hh in range(COL_TILE // DN_DK):
            th = t[:, hh * DN_DK:(hh + 1) * DN_DK]
            outs.append(th * lax.rsqrt(jnp.sum(th * th, axis=-1, keepdims=True) + 1e-6))
        return jnp.concatenate(outs, axis=1)

    def rows(c):
        return pl.multiple_of(c * chunk, chunk)

    @pl.when(j == 0)
    def _():
        def body(c, carry):
            r0 = rows(c)
            t = l2norm_heads(_silu(conv_rows(r0, dnc_ref, DN_CONV)))
            main_ref[pl.ds(r0, chunk), :] = t * (DN_DK ** -0.5)
            return carry
        lax.fori_loop(0, n_chunks, body, 0)

    @pl.when(j == 1)
    def _():
        def body(c, carry):
            r0 = rows(c)
            main_ref[pl.ds(r0, chunk), :] = l2norm_heads(_silu(conv_rows(r0, dnc_ref, DN_CONV)))
            return carry
        lax.fori_loop(0, n_chunks, body, 0)

    @pl.when(j == 2)
    def _():
        def body(c, carry):
            r0 = rows(c)
            main_ref[pl.ds(r0, chunk), :] = _silu(conv_rows(r0, dnc_ref, DN_CONV))
            return carry
        lax.fori_loop(0, n_chunks, body, 0)

    @pl.when(j == 3)
    def _():
        def body(c, carry):
            r0 = rows(c)
            main_ref[pl.ds(r0, chunk), :] = _silu(cbuf_ref[pl.ds(r0 + pad, chunk), :])
            return carry
        lax.fori_loop(0, n_chunks, body, 0)

    @pl.when(j >= HY_COL0)
    def _():
        def body(c, carry):
            r0 = rows(c)
            main_ref[pl.ds(r0, chunk), :] = conv_rows(r0, hyc_ref, HY_SHORT) + hyb_ref[...]
            return carry
        lax.fori_loop(0, n_chunks, body, 0)


def _mixin(layer, x, norm_w, w_main, w_ba, dn_conv, hy_conv, hy_bias, a_log, dt_bias):
    b, seq, _ = x.shape
    chunk = DELTA_CHUNK
    n_chunks = seq // chunk
    n_bd = N_DIRS * DN_HEADS
    dn_tiles = (2 * DN_KEY + DN_VAL) // COL_TILE
    hy_tiles = 3 * HY_WIDTH // COL_TILE
    grid_spec = pltpu.PrefetchScalarGridSpec(
        num_scalar_prefetch=1,
        grid=(b, N_COL_TILES),
        in_specs=[
            pl.BlockSpec((None, seq, D_MODEL), lambda i, j, l: (i, 0, 0)),
            pl.BlockSpec((None, 1, D_MODEL), lambda i, j, l: (l[0], 0, 0)),
            pl.BlockSpec((None, D_MODEL, COL_TILE), lambda i, j, l: (l[0], 0, j)),
            pl.BlockSpec((None, D_MODEL, LANES), lambda i, j, l: (l[0], 0, 0)),
            pl.BlockSpec((None, DN_CONV, COL_TILE),
                         lambda i, j, l: (l[0], 0, jnp.minimum(j, dn_tiles - 1))),
            pl.BlockSpec((None, HY_SHORT, COL_TILE),
                         lambda i, j, l: (l[0], 0, jnp.clip(j - HY_COL0, 0, hy_tiles - 1))),
            pl.BlockSpec((None, 1, COL_TILE),
                         lambda i, j, l: (l[0], 0, jnp.clip(j - HY_COL0, 0, hy_tiles - 1))),
            pl.BlockSpec((None, 1, LANES), lambda i, j, l: (l[0], 0, 0)),
            pl.BlockSpec((None, 1, LANES), lambda i, j, l: (l[0], 0, 0)),
        ],
        out_specs=[
            pl.BlockSpec((None, seq, COL_TILE), lambda i, j, l: (i, 0, j)),
            pl.BlockSpec((None, 3, seq, LANES), lambda i, j, l: (i, 0, 0, 0)),
            pl.BlockSpec((None, n_chunks, n_bd, chunk), lambda i, j, l: (i, 0, 0, 0)),
        ],
        scratch_shapes=[pltpu.VMEM((seq, D_MODEL), BF16),
                        pltpu.VMEM((seq + 2 * SUBLANES, COL_TILE), F32)],
    )
    return pl.pallas_call(
        functools.partial(_mixin_body, seq=seq, chunk=chunk), grid_spec=grid_spec,
        out_shape=[jax.ShapeDtypeStruct((b, seq, N_MAIN), F32),
                   jax.ShapeDtypeStruct((b, 3, seq, LANES), F32),
                   jax.ShapeDtypeStruct((b, n_chunks, n_bd, chunk), F32)],
        compiler_params=_params(2), name="mixin",
    )(layer, x, norm_w, w_main, w_ba, dn_conv, hy_conv, hy_bias, a_log, dt_bias)


_INV_LEVELS = (8, 16, 32, 64, 128)


def _delta_body(layer_ref, q_ref, k_ref, v_ref, z_ref, st_ref, gct_ref, nw_ref, o_ref,
                wq_s, u_s, att_s, kdt_s, gam_s, s_s, oacc_s, *, seq):
    del layer_ref
    hp = pl.program_id(1)
    heads = range(HEADS_PER_STEP)
    c_ = DELTA_CHUNK
    n_chunks = seq // c_
    n_bd = N_DIRS * DN_HEADS
    row = lax.broadcasted_iota(jnp.int32, (c_, c_), 0)
    col = lax.broadcasted_iota(jnp.int32, (c_, c_), 1)
    lane = lax.broadcasted_iota(jnp.int32, (c_, LANES), 1)

    half_row = lax.broadcasted_iota(jnp.int32, (c_ // 2, c_), 0)
    half_col = lax.broadcasted_iota(jnp.int32, (c_ // 2, c_), 1)

    def pick(x, idx):
        return jnp.sum(jnp.where(lane == idx, x, 0.0), axis=1, keepdims=True)

    par = math.gcd(max(PAR_CHUNKS // HEADS_PER_STEP, 1), n_chunks)

    def phase_a(it, carry):
        chains, cells = [], []
        for hh, j in itertools.product(heads, range(par)):
            hd = hp * HEADS_PER_STEP + hh
            head_lanes = slice(hh * DN_DK, (hh + 1) * DN_DK)
            g = it * par + j
            r0 = pl.multiple_of(g * c_, c_)
            q = q_ref[pl.ds(r0, c_), head_lanes]
            k = k_ref[pl.ds(r0, c_), head_lanes]
            v = v_ref[pl.ds(r0, c_), head_lanes]
            st0 = st_ref[0, pl.ds(r0, c_), :]
            st1 = st_ref[1, pl.ds(r0, c_), :]
            st2 = st_ref[2, pl.ds(r0, c_), :]
            cell = types.SimpleNamespace(k_bf=k.astype(BF16), kb_bf=[], chains=[])
            cells.append(cell)
            for d in range(N_DIRS):
                ch = types.SimpleNamespace(d=d, step=g if d == 0 else n_chunks - 1 - g, hh=hh,
                                           lanes=slice(d * DN_DV, (d + 1) * DN_DV))
                cell.chains.append(ch)
                bi = d * DN_HEADS + hd
                beta = pick(st0, bi)
                gc = pick(st0, n_bd + bi)
                egc = jnp.exp(gc)
                ekd = jnp.exp(pick(st1, n_bd + bi))
                egl = jnp.exp(pick(st2, n_bd + bi))
                gcr = gct_ref[g, pl.ds(bi, 1), :]
                incl = (row >= col) if d == 0 else (row <= col)
                ch.strict = (row > col) if d == 0 else (row < col)
                ch.dec = jnp.where(incl, jnp.exp(jnp.where(incl, gc - gcr, 0.0)), 0.0)
                kb = k * beta
                cell.kb_bf.append(kb.astype(BF16))
                ch.rhs = jnp.concatenate([(v * beta).astype(BF16), (kb * egc).astype(BF16)], axis=1)
                wq_s[hh, ch.step, c_:2 * c_, ch.lanes] = (q * egc).astype(BF16)
                kdt_s[hh, ch.step, d * DN_DK:(d + 1) * DN_DK, :] = (k * ekd).T.astype(BF16)
                gam_s[hh, ch.step, d * DN_DK:(d + 1) * DN_DK, :] = jnp.broadcast_to(
                    egl[0:DN_DK, :], (DN_DK, N_DIRS * DN_DV))
                chains.append(ch)
            cell.lhs = jnp.concatenate(cell.kb_bf + [q.astype(BF16)], axis=0)

        for cell in cells:
            s1 = lax.dot_general(cell.lhs, cell.k_bf, (((1,), (1,)), ((), ())),
                                 preferred_element_type=F32)
            for ch in cell.chains:
                ch.kk = s1[ch.d * c_:(ch.d + 1) * c_]
                ch.qk = s1[N_DIRS * c_:]
        for ch in chains:
            a = jnp.where(ch.strict, ch.kk * ch.dec, 0.0)
            att_s[ch.hh, ch.step, :, ch.d * c_:(ch.d + 1) * c_] = (ch.qk * ch.dec).astype(BF16)
            ch.a = a
            ch.b0 = jnp.where((row >> 3) == (col >> 3), -a, 0.0)
            ch.b0_bf = ch.b0.astype(BF16)
        for ch in chains:
            ch.r1 = _dot(ch.b0_bf, ch.b0_bf)
        for ch in chains:
            r1_bf = ch.r1.astype(BF16)
            ch.m = _dot(jnp.concatenate([ch.b0_bf, r1_bf], axis=0), r1_bf)
        for ch in chains:
            ch.q2 = ch.b0 + ch.r1 + ch.m[:c_]
            ch.r2 = ch.m[c_:]
            ch.q2r2 = _dot(ch.q2.astype(BF16), ch.r2.astype(BF16))
        for ch in chains:
            ch.xf = jnp.where(row == col, 1.0, ch.q2 + ch.r2 + ch.q2r2)
            ch.x = ch.xf.astype(BF16)
        for s in _INV_LEVELS:
            sh = s.bit_length() - 1
            n_pairs = c_ // (2 * s)

            def take(m, odd, s=s, n_pairs=n_pairs):
                return jnp.concatenate(
                    [m[(2 * i + odd) * s:(2 * i + odd + 1) * s] for i in range(n_pairs)], axis=0)

            def interleave(even, odd, s=s, n_pairs=n_pairs):
                pieces = []
                for i in range(n_pairs):
                    pieces += [even[i * s:(i + 1) * s], odd[i * s:(i + 1) * s]]
                return jnp.concatenate(pieces, axis=0)

            for ch in chains:
                act = 1 - ch.d
                partner_cols = (half_col >> sh) == (((half_row >> sh) << 1) + ch.d)
                e = jnp.where(partner_cols, take(ch.a, act), 0.0)
                ch.m1 = _dot(e.astype(BF16), ch.x)
            for ch in chains:
                act = 1 - ch.d
                zeros = jnp.zeros_like(ch.m1)
                m1_rows = interleave(zeros, -ch.m1) if act else interleave(-ch.m1, zeros)
                ch.m2 = _dot(take(ch.xf, act).astype(BF16), m1_rows.astype(BF16))
            for ch in chains:
                act = 1 - ch.d
                new = take(ch.xf, act) + ch.m2
                keep = take(ch.xf, ch.d)
                ch.xf = interleave(keep, new) if act else interleave(new, keep)
                ch.x = ch.xf.astype(BF16)
        for ch in chains:
            ch.uw = _dot(ch.x, ch.rhs)
        for ch in chains:
            u_s[ch.hh, ch.step, :, ch.lanes] = ch.uw[:, :DN_DV]
            wq_s[ch.hh, ch.step, 0:c_, ch.lanes] = ch.uw[:, DN_DV:].astype(BF16)
        return carry

    lax.fori_loop(0, n_chunks // par, phase_a, 0)

    s_s[...] = jnp.zeros_like(s_s)
    oacc_s[...] = jnp.zeros_like(oacc_s)
    w2 = N_DIRS * DN_DV
    lane2 = lax.broadcasted_iota(jnp.int32, (c_, w2), 1)
    blockdiag = (lax.broadcasted_iota(jnp.int32, (w2, w2), 0) >> 7) == (
        lax.broadcasted_iota(jnp.int32, (w2, w2), 1) >> 7)

    def scan_step(s, carry):
        rf = pl.multiple_of(s * c_, c_)
        rb = pl.multiple_of((n_chunks - 1 - s) * c_, c_)
        wqs = [_dot(wq_s[hh, s], s_s[hh].astype(BF16)) for hh in heads]
        vn = [u_s[hh, s] - wqs[hh][:c_] for hh in heads]
        kv = [_dot(kdt_s[hh, s], vn[hh].astype(BF16)) for hh in heads]
        for hh in heads:
            s_s[hh] = s_s[hh] * gam_s[hh, s] + jnp.where(blockdiag, kv[hh], 0.0)
        for hh in heads:
            rhs = jnp.concatenate([jnp.where(lane2 < DN_DV, vn[hh], 0.0).astype(BF16),
                                   jnp.where(lane2 >= DN_DV, vn[hh], 0.0).astype(BF16)], axis=0)
            o = wqs[hh][c_:] + _dot(att_s[hh, s], rhs)
            oacc_s[hh, pl.ds(rf, c_), :] += o[:, :DN_DV]
            oacc_s[hh, pl.ds(rb, c_), :] += o[:, DN_DV:]
        return carry

    lax.fori_loop(0, n_chunks, scan_step, 0)

    for hh in heads:
        head_lanes = slice(hh * DN_DV, (hh + 1) * DN_DV)
        o = oacc_s[hh]
        ms = jnp.mean(o * o, axis=-1, keepdims=True)
        o_ref[:, head_lanes] = o * lax.rsqrt(ms + RMS_EPS) * nw_ref[...] * z_ref[:, head_lanes]


def _delta(layer, main, st, gct, norm_w):
    b, seq, _ = main.shape
    c_ = DELTA_CHUNK
    n_chunks = seq // c_
    n_bd = N_DIRS * DN_HEADS
    w2 = N_DIRS * DN_DV
    hps = HEADS_PER_STEP
    groups = DN_HEADS // hps
    head_spec = lambda part: pl.BlockSpec(
        (None, seq, hps * DN_DK), lambda i, h, l: (i, 0, part * groups + h))
    grid_spec = pltpu.PrefetchScalarGridSpec(
        num_scalar_prefetch=1,
        grid=(b, groups),
        in_specs=[
            head_spec(0), head_spec(1), head_spec(2), head_spec(3),
            pl.BlockSpec((None, 3, seq, LANES), lambda i, h, l: (i, 0, 0, 0)),
            pl.BlockSpec((None, n_chunks, n_bd, c_), lambda i, h, l: (i, 0, 0, 0)),
            pl.BlockSpec((None, 1, DN_DV), lambda i, h, l: (l[0], 0, 0)),
        ],
        out_specs=pl.BlockSpec((None, seq, hps * DN_DV), lambda i, h, l: (i, 0, h)),
        scratch_shapes=[
            pltpu.VMEM((hps, n_chunks, 2 * c_, w2), BF16),
            pltpu.VMEM((hps, n_chunks, c_, w2), F32),
            pltpu.VMEM((hps, n_chunks, c_, N_DIRS * c_), BF16),
            pltpu.VMEM((hps, n_chunks, N_DIRS * DN_DK, c_), BF16),
            pltpu.VMEM((hps, n_chunks, N_DIRS * DN_DK, w2), F32),
            pltpu.VMEM((hps, N_DIRS * DN_DK, w2), F32),
            pltpu.VMEM((hps, seq, DN_DV), F32),
        ],
    )
    return pl.pallas_call(
        functools.partial(_delta_body, seq=seq), grid_spec=grid_spec,
        out_shape=jax.ShapeDtypeStruct((b, seq, DN_VAL), F32),
        compiler_params=_params(2), name="delta",
    )(layer, main, main, main, main, st, gct, norm_w)


def _dft_tables(seq):
    n = 2 * seq
    k = jnp.arange(seq, dtype=jnp.int32)[:, None]

    def trig(t):
        ang = ((k * t[None, :]) & (n - 1)).astype(F32) * (2.0 * math.pi / n)
        return jnp.cos(ang), jnp.sin(ang)

    c_hi, s_hi = trig(jnp.arange(seq // LANES, dtype=jnp.int32) * LANES)
    c_lo, s_lo = trig(jnp.arange(LANES, dtype=jnp.int32))
    c_hi, s_hi = c_hi[:, :, None], s_hi[:, :, None]
    c_lo, s_lo = c_lo[:, None, :], s_lo[:, None, :]
    cos = (c_hi * c_lo - s_hi * s_lo).reshape(seq, seq)
    sin = (s_hi * c_lo + c_hi * s_lo).reshape(seq, seq)
    return cos.astype(BF16), sin.astype(BF16)


def _filter_feats(seq):
    t = jnp.linspace(0.0, 1.0, seq, dtype=F32)[:, None]
    bands = (HY_EMB - 1) // 2
    ang = ((2.0 * math.pi / seq) * jnp.arange(seq, dtype=F32)[:, None]
           * jnp.linspace(1e-4, bands - 1, bands, dtype=F32)[None, :])
    feats = jnp.concatenate([t, jnp.cos(ang), -jnp.sin(ang)], axis=-1)
    return jnp.pad(feats, ((0, 0), (0, PAD_HID - HY_EMB)))


def _filter_deltas():
    max_decay = math.log(HY_DECAY_TARGET) / HY_FAST_DECAY_PCT
    min_decay = math.log(HY_DECAY_TARGET) / HY_SLOW_DECAY_PCT
    return jnp.abs(jnp.linspace(min_decay, max_decay, HY_WIDTH, dtype=F32))[None, :]


def _filt_body(layer_ref, feats_ref, w1_ref, b1_ref, w2_ref, b2_ref, w3_ref, b3_ref, fr_ref,
               wof_ref, wob_ref, dl_ref, c_ref, s_ref, kre_ref, kim_ref, hdn_ref, *, seq):
    del layer_ref
    n = 2 * seq

    @pl.when(pl.program_id(0) == 0)
    def _():
        fr = fr_ref[...]
        hp = functools.partial(jnp.dot, preferred_element_type=F32, precision=HIGHEST)
        h = jnp.sin(fr * (hp(feats_ref[...], w1_ref[...]) + b1_ref[...]))
        h = jnp.sin(fr * (hp(h, w2_ref[...]) + b2_ref[...]))
        hdn_ref[...] = jnp.sin(fr * (hp(h, w3_ref[...]) + b3_ref[...]))

    rowi = lax.broadcasted_iota(jnp.int32, (seq, HY_CT), 0)
    tt = rowi.astype(F32) * (1.0 / (seq - 1))
    window = jnp.exp(-tt * dl_ref[...])
    def split(a):
        hi = a.astype(BF16)
        return hi, (a - hi.astype(F32)).astype(BF16)

    h_hi, h_lo = split(hdn_ref[...])

    def filt(w_ref):
        w_hi, w_lo = split(w_ref[...])
        return _dot(h_hi, w_hi) + (_dot(h_hi, w_lo) + _dot(h_lo, w_hi))

    fwd = filt(wof_ref) * window
    bwd = filt(wob_ref) * window
    bsh = jnp.where(rowi >= 1, pltpu.roll(bwd, 1, axis=0), 0.0)
    both = fwd + bsh
    kre = _dot(c_ref[...], both.astype(BF16))
    kim = _dot(s_ref[...], (bsh - fwd).astype(BF16))
    alt = (1 - 2 * (rowi & 1)).astype(F32)
    nyq = jnp.sum(alt * both, axis=0, keepdims=True)
    first = rowi == 0
    kre_ref[...] = kre * jnp.where(first, 1.0 / n, 2.0 / n)
    kim_ref[...] = jnp.where(first, nyq * (1.0 / n), kim * (2.0 / n))


def _filter_spectrum(layer, feats, w1, b1, w2, b2, w3, b3, freq, wout, deltas, ctab, stab):
    seq = feats.shape[0]
    n_oc = HY_ORDER * HY_WIDTH
    tiles = n_oc // HY_CT
    per_w = HY_WIDTH // HY_CT
    sq = lambda: pl.BlockSpec((None, PAD_HID, PAD_HID), lambda j, l: (l[0], 0, 0))
    vec = lambda: pl.BlockSpec((None, 1, PAD_HID), lambda j, l: (l[0], 0, 0))
    grid_spec = pltpu.PrefetchScalarGridSpec(
        num_scalar_prefetch=1,
        grid=(tiles,),
        in_specs=[
            pl.BlockSpec((seq, PAD_HID), lambda j, l: (0, 0)),
            sq(), vec(), sq(), vec(), sq(), vec(), vec(),
            pl.BlockSpec((None, PAD_HID, HY_CT), lambda j, l: (l[0], 0, j)),
            pl.BlockSpec((None, PAD_HID, HY_CT), lambda j, l: (l[0], 0, tiles + j)),
            pl.BlockSpec((1, HY_CT), lambda j, l: (0, j % per_w)),
            _resident((seq, seq), lambda j, l: (0, 0)),
            _resident((seq, seq), lambda j, l: (0, 0)),
        ],
        out_specs=[pl.BlockSpec((None, seq, HY_CT), lambda j, l: (j % per_w, 0, j // per_w)),
                   pl.BlockSpec((None, seq, HY_CT), lambda j, l: (j % per_w, 0, j // per_w))],
        scratch_shapes=[pltpu.VMEM((seq, PAD_HID), F32)],
    )
    return pl.pallas_call(
        functools.partial(_filt_body, seq=seq), grid_spec=grid_spec,
        out_shape=[jax.ShapeDtypeStruct((per_w, seq, HY_ORDER * HY_CT), F32)] * 2,
        compiler_params=_params(1), name="hyena_filter",
    )(layer, feats, w1, b1, w2, b2, w3, b3, freq, wout, wout, deltas, ctab, stab)


def _hyena_body(layer_ref, v_ref, x1_ref, x2_ref, kre_ref, kim_ref, skip_ref, c_ref, s_ref, o_ref,
                z_s, xc_s, xs_s, ya_s, yb_s, *, seq):
    del layer_ref
    rc = HY_ROWS
    n_rc = seq // rc
    alt = (1 - 2 * (lax.broadcasted_iota(jnp.int32, (rc, HY_CT), 0) & 1)).astype(F32)

    def rows(c):
        return pl.ds(pl.multiple_of(c * rc, rc), rc)

    for o, gate_ref in enumerate((x1_ref, x2_ref)):
        z_ref = v_ref if o == 0 else z_s
        last = o == HY_ORDER - 1
        cols = slice(o * HY_CT, (o + 1) * HY_CT)

        def to_bf16(c, acc):
            z = z_ref[rows(c), :]
            ya_s[rows(c), :] = z.astype(BF16)
            yb_s[rows(c), :] = z.astype(BF16)
            return acc + jnp.sum(alt * z, axis=0, keepdims=True)

        xnyq = lax.fori_loop(0, n_rc, to_bf16, jnp.zeros((1, HY_CT), F32))
        xc_s[...] = _dot(c_ref[...], ya_s[...])
        xs_s[...] = _dot(s_ref[...], yb_s[...])

        def spectrum(c, carry):
            xc, xs = xc_s[rows(c), :], xs_s[rows(c), :]
            kre, kim = kre_ref[rows(c), cols], kim_ref[rows(c), cols]
            ya_s[rows(c), :] = (xc * kre + xs * kim).astype(BF16)
            yb_s[rows(c), :] = (xs * kre - xc * kim).astype(BF16)
            return carry

        lax.fori_loop(0, n_rc, spectrum, 0)
        xc_s[...] = _dot(c_ref[...], ya_s[...])
        xs_s[...] = _dot(s_ref[...], yb_s[...])
        ynyq = xnyq * kim_ref[0:1, cols]
        skip = skip_ref[o:o + 1, :]

        def gate(c, carry):
            z = z_ref[rows(c), :]
            y = xc_s[rows(c), :] + xs_s[rows(c), :] + alt * ynyq
            z_new = gate_ref[rows(c), :] * (y + skip * z)
            (o_ref if last else z_s)[rows(c), :] = z_new
            return carry

        lax.fori_loop(0, n_rc, gate, 0)


def _hyena(layer, main, kre, kim, skip, ctab, stab):
    b, seq, _ = main.shape
    tiles = HY_WIDTH // HY_CT
    col0 = (2 * DN_KEY + 2 * DN_VAL) // HY_CT
    hy_spec = lambda part: pl.BlockSpec(
        (None, seq, HY_CT), lambda j, i, l: (i, 0, col0 + part * tiles + j))
    k_spec = lambda: pl.BlockSpec((None, seq, HY_ORDER * HY_CT), lambda j, i, l: (j, 0, 0),
                                  pipeline_mode=pl.Buffered(1))
    grid_spec = pltpu.PrefetchScalarGridSpec(
        num_scalar_prefetch=1,
        grid=(tiles, b),
        in_specs=[
            hy_spec(0), hy_spec(1), hy_spec(2), k_spec(), k_spec(),
            pl.BlockSpec((None, None, HY_ORDER, HY_CT), lambda j, i, l: (l[0], j, 0, 0)),
            _resident((seq, seq), lambda j, i, l: (0, 0)),
            _resident((seq, seq), lambda j, i, l: (0, 0)),
        ],
        out_specs=pl.BlockSpec((None, seq, HY_CT), lambda j, i, l: (i, 0, j)),
        scratch_shapes=[pltpu.VMEM((seq, HY_CT), F32),
                        pltpu.VMEM((seq, HY_CT), F32), pltpu.VMEM((seq, HY_CT), F32),
                        pltpu.VMEM((seq, HY_CT), BF16), pltpu.VMEM((seq, HY_CT), BF16)],
    )
    return pl.pallas_call(
        functools.partial(_hyena_body, seq=seq), grid_spec=grid_spec,
        out_shape=jax.ShapeDtypeStruct((b, seq, HY_WIDTH), F32),
        compiler_params=_params(2), name="hyena",
    )(layer, main, main, main, kre, kim, skip, ctab, stab)


def _mixout_body(layer_ref, x_ref, odn_ref, zz_ref, hn_ref, wo_ref, o_ref):
    del layer_ref
    zz = zz_ref[...]
    ms = jnp.mean(zz * zz, axis=-1, keepdims=True)
    zn = zz * lax.rsqrt(ms + RMS_EPS) * hn_ref[...]
    y = _dot(odn_ref[...].astype(BF16), wo_ref[0:DN_VAL, :])
    y = y + _dot(zn.astype(BF16), wo_ref[DN_VAL:D_MODEL, :])
    o_ref[...] = x_ref[...] + y


def _mixout(layer, x2d, odn2d, zz2d, hy_norm, w_out, *, tm):
    m = x2d.shape[0]
    grid_spec = pltpu.PrefetchScalarGridSpec(
        num_scalar_prefetch=1,
        grid=(m // tm,),
        in_specs=[
            pl.BlockSpec((tm, D_MODEL), lambda i, l: (i, 0)),
            pl.BlockSpec((tm, DN_VAL), lambda i, l: (i, 0)),
            pl.BlockSpec((tm, HY_WIDTH), lambda i, l: (i, 0)),
            pl.BlockSpec((None, 1, HY_WIDTH), lambda i, l: (l[0], 0, 0)),
            pl.BlockSpec((None, D_MODEL, D_MODEL), lambda i, l: (l[0], 0, 0)),
        ],
        out_specs=pl.BlockSpec((tm, D_MODEL), lambda i, l: (i, 0)),
    )
    return pl.pallas_call(
        _mixout_body, grid_spec=grid_spec,
        out_shape=jax.ShapeDtypeStruct((m, D_MODEL), F32),
        compiler_params=_params(1), name="mixout",
    )(layer, x2d, odn2d, zz2d, hy_norm, w_out)


def _final_norm_body(x_ref, w_ref, o_ref):
    x = x_ref[...]
    ms = jnp.mean(x * x, axis=-1, keepdims=True)
    o_ref[...] = x * lax.rsqrt(ms + RMS_EPS) * w_ref[...]


def _final_norm(x2d, w, *, tm):
    m = x2d.shape[0]
    return pl.pallas_call(
        _final_norm_body, grid=(m // tm,),
        in_specs=[pl.BlockSpec((tm, D_MODEL), lambda i: (i, 0)),
                  pl.BlockSpec((1, D_MODEL), lambda i: (0, 0))],
        out_specs=pl.BlockSpec((tm, D_MODEL), lambda i: (i, 0)),
        out_shape=jax.ShapeDtypeStruct((m, D_MODEL), F32),
        compiler_params=_params(1), name="final_norm",
    )(x2d, w)


def _pad_to(a, axis, size):
    pads = [(0, 0)] * a.ndim
    pads[axis] = (0, size - a.shape[axis])
    return jnp.pad(a, pads)


def _row_tile(m):
    return 1024 if m % 1024 == 0 else m


def kernel(x, ffn1_norm, ffn1_w_gate, ffn1_w_up, ffn1_w_down, mix_norm, w_in, dn_conv, dn_a_log, dn_dt_bias, dn_norm, hy_conv, hy_conv_bias, hy_f_w1, hy_f_b1, hy_f_w2, hy_f_b2, hy_f_w3, hy_f_b3, hy_f_freq, hy_f_wout, hy_skip, hy_norm, w_out, ffn2_norm, ffn2_w_gate, ffn2_w_up, ffn2_w_down, final_norm):
    b, seq, _ = x.shape
    m = b * seq
    tm = _row_tile(m)
    n_bd = N_DIRS * DN_HEADS
    bf = lambda a: a.astype(BF16)
    row3 = lambda a: a[:, None, :]

    ba0 = 2 * DN_KEY + 2 * DN_VAL
    w_in_bf = bf(w_in)
    w_main = jnp.concatenate([w_in_bf[:, :, :ba0], w_in_bf[:, :, ba0 + 2 * n_bd:]], axis=-1)
    w_ba = _pad_to(w_in_bf[:, :, ba0:ba0 + 2 * n_bd], 2, LANES)
    lane_params = lambda p: _pad_to(
        jnp.pad(p.reshape(DEPTH, 1, n_bd), ((0, 0), (0, 0), (n_bd, 0))), 2, LANES)
    a_log = lane_params(dn_a_log)
    dt_bias = lane_params(dn_dt_bias)
    f_w1 = _pad_to(_pad_to(hy_f_w1, 1, PAD_HID), 2, PAD_HID)
    f_w2 = _pad_to(_pad_to(hy_f_w2, 1, PAD_HID), 2, PAD_HID)
    f_w3 = _pad_to(_pad_to(hy_f_w3, 1, PAD_HID), 2, PAD_HID)
    f_b1, f_b2, f_b3, f_fr = (row3(_pad_to(p, 1, PAD_HID))
                              for p in (hy_f_b1, hy_f_b2, hy_f_b3, hy_f_freq))
    f_wout = _pad_to(hy_f_wout, 1, PAD_HID)
    tiles = HY_WIDTH // HY_CT
    skip = hy_skip.reshape(DEPTH, HY_ORDER, tiles, HY_CT).transpose(0, 2, 1, 3)
    weights = dict(
        ffn1=(row3(ffn1_norm), bf(ffn1_w_gate), bf(ffn1_w_up), bf(ffn1_w_down)),
        ffn2=(row3(ffn2_norm), bf(ffn2_w_gate), bf(ffn2_w_up), bf(ffn2_w_down)),
        w_out=bf(w_out))

    ctab, stab = lax.optimization_barrier(_dft_tables(seq))
    feats = _filter_feats(seq)
    deltas = _filter_deltas()

    def layer_fn(i, carry):
        xc, ctab, stab = carry
        layer = jnp.full((1,), i, jnp.int32)
        x2 = _ffn(layer, xc.reshape(m, D_MODEL), *weights["ffn1"], tm=tm)
        x3 = x2.reshape(b, seq, D_MODEL)
        main, st, gct = _mixin(layer, x3, row3(mix_norm), w_main, w_ba, dn_conv, hy_conv,
                               row3(hy_conv_bias), a_log, dt_bias)
        kre, kim = _filter_spectrum(layer, feats, f_w1, f_b1, f_w2, f_b2, f_w3, f_b3, f_fr,
                                    f_wout, deltas, ctab, stab)
        o_dn = _delta(layer, main, st, gct, row3(dn_norm))
        zz = _hyena(layer, main, kre, kim, skip, ctab, stab)
        x4 = _mixout(layer, x2, o_dn.reshape(m, DN_VAL), zz.reshape(m, HY_WIDTH),
                     row3(hy_norm), weights["w_out"], tm=tm)
        x5 = _ffn(layer, x4, *weights["ffn2"], tm=tm)
        return x5.reshape(b, seq, D_MODEL), ctab, stab

    xf, _, _ = lax.fori_loop(0, DEPTH, layer_fn, (x, ctab, stab))
    return _final_norm(xf.reshape(m, D_MODEL), final_norm[None, :], tm=tm).reshape(b, seq, D_MODEL)
```
